```python
import jax, jax.numpy as jnp
from jax import lax
import numpy as np

D_MODEL = 1024
BATCH = 2
SEQ = 8192
DEPTH = 4
DEC_BATCH = 4
DEC_SEQ = 8192
PAST_LEN = 128

HEAD_DIM = 64
GRID_W = 64
NA_HEADS = 4
WIN_R = 8
WIN_C = 16
SG_GROUPS = 4
SG_GROUP_DIM = 64
SG_CHUNK = 128
GQA_Q_HEADS = 8
GQA_KV_HEADS = 2
Q_BLOCK = 128
ROPE_THETA = 10000.0
D_FF = 2816
EPS = 1e-6

NA_WIDTH = NA_HEADS * HEAD_DIM
SG_WIDTH = SG_GROUPS * SG_GROUP_DIM
GQA_Q_WIDTH = GQA_Q_HEADS * HEAD_DIM
GQA_KV_WIDTH = GQA_KV_HEADS * HEAD_DIM
D_MIX = NA_WIDTH + SG_WIDTH + GQA_Q_WIDTH
IN_SIZES = (NA_WIDTH, NA_WIDTH, NA_WIDTH, SG_WIDTH, SG_WIDTH, GQA_Q_WIDTH, GQA_KV_WIDTH, GQA_KV_WIDTH)
D_IN = sum(IN_SIZES)
IN_SPLITS = tuple(int(c) for c in np.cumsum(IN_SIZES)[:-1])

kernel_name = "hybrid_bidir_encoder_na_gmlp_gqa"


def rms_norm(x, g):
    xf = x.astype(jnp.float32)
    y = xf * lax.rsqrt(jnp.mean(xf * xf, axis=-1, keepdims=True) + EPS)
    return (y * g.astype(jnp.float32)).astype(x.dtype)


def swiglu_ffn(x, w_gate, w_up, w_down):
    return (jax.nn.silu(x @ w_gate) * (x @ w_up)) @ w_down


def neighbourhood_attention(q, k, v, rpb):
    b, s, h, d = q.shape
    rows = s // GRID_W
    wr = min(WIN_R, rows)
    q = q.reshape(b, rows, GRID_W, h, d)
    k = k.reshape(b, rows, GRID_W, h, d)
    v = v.reshape(b, rows, GRID_W, h, d)
    row_start = jnp.clip(jnp.arange(rows) - wr // 2, 0, rows - wr)
    cols = jnp.arange(GRID_W)
    col_start = jnp.clip(cols - WIN_C // 2, 0, GRID_W - WIN_C)
    col_idx = col_start[:, None] + jnp.arange(WIN_C)
    dc = col_idx - cols[:, None] + (WIN_C - 1)
    scale = HEAD_DIM ** -0.5

    def one_row(r):
        rs = row_start[r]
        q_r = lax.dynamic_index_in_dim(q, r, axis=1, keepdims=False)
        k_rows = lax.dynamic_slice_in_dim(k, rs, wr, axis=1)
        v_rows = lax.dynamic_slice_in_dim(v, rs, wr, axis=1)
        k_g = k_rows[:, :, col_idx]
        v_g = v_rows[:, :, col_idx]
        dr = rs + jnp.arange(wr) - r + (WIN_R - 1)
        bias = rpb[:, dr][:, :, dc].transpose(0, 2, 1, 3)
        sc = jnp.einsum('bchd,bicjhd->bhcij', q_r, k_g).astype(jnp.float32) * scale
        sc = sc + bias.astype(jnp.float32)[None]
        p = jax.nn.softmax(sc.reshape(b, h, GRID_W, wr * WIN_C), axis=-1)
        p = p.reshape(b, h, GRID_W, wr, WIN_C).astype(v.dtype)
        return jnp.einsum('bhcij,bicjhd->bchd', p, v_g)

    out = lax.map(one_row, jnp.arange(rows))
    return out.transpose(1, 0, 2, 3, 4).reshape(b, s, h * d)


def spatial_gating(u, v, v_norm, w_s, b_s):
    b, s, _ = u.shape
    n = s // SG_CHUNK
    u = jax.nn.gelu(u)
    v = rms_norm(jax.nn.gelu(v).reshape(b, s, SG_GROUPS, SG_GROUP_DIM), v_norm)
    vc = v.reshape(b, n, SG_CHUNK, SG_GROUPS, SG_GROUP_DIM)
    mixed = jnp.einsum('gpq,bnqgc->bnpgc', w_s, vc) + b_s.T[:, :, None]
    return u * mixed.reshape(b, s, SG_WIDTH)


def axial_rope_tables(s):
    t = jnp.arange(s)
    row = (t // GRID_W).astype(jnp.float32)
    col = (t % GRID_W).astype(jnp.float32)
    n_freq = HEAD_DIM // 4
    inv = ROPE_THETA ** (-jnp.arange(n_freq, dtype=jnp.float32) / n_freq)
    ang = jnp.concatenate([row[:, None] * inv, col[:, None] * inv], axis=-1)
    return jnp.cos(ang), jnp.sin(ang)


def apply_rope(x, cos, sin):
    xf = x.astype(jnp.float32)
    half = HEAD_DIM // 2
    x1, x2 = xf[..., :half], xf[..., half:]
    c, s_ = cos[None, :, None, :], sin[None, :, None, :]
    return jnp.concatenate([x1 * c - x2 * s_, x2 * c + x1 * s_], axis=-1).astype(x.dtype)


def gqa_attention(q, k, v):
    b, s, _, d = q.shape
    nb = s // Q_BLOCK
    rep = GQA_Q_HEADS // GQA_KV_HEADS
    scale = HEAD_DIM ** -0.5
    qb = q.reshape(b, nb, Q_BLOCK, GQA_KV_HEADS, rep, d).transpose(1, 0, 2, 3, 4, 5)

    def one_block(q_blk):
        sc = jnp.einsum('bqgrd,bkgd->bgrqk', q_blk, k).astype(jnp.float32) * scale
        p = jax.nn.softmax(sc, axis=-1).astype(v.dtype)
        return jnp.einsum('bgrqk,bkgd->bqgrd', p, v)

    out = lax.map(one_block, qb)
    return out.transpose(1, 0, 2, 3, 4, 5).reshape(b, s, GQA_Q_HEADS * d)


def trunk(x, ffn1_norm, ffn1_w_gate, ffn1_w_up, ffn1_w_down, mix_norm, w_in,
          na_q_norm, na_k_norm, na_rpb, sg_v_norm, sg_w, sg_b, gqa_q_norm, gqa_k_norm,
          w_out, ffn2_norm, ffn2_w_gate, ffn2_w_up, ffn2_w_down):
    b, s, _ = x.shape
    cos, sin = axial_rope_tables(s)
    for l in range(DEPTH):
        x = x + 0.5 * swiglu_ffn(rms_norm(x, ffn1_norm[l]), ffn1_w_gate[l], ffn1_w_up[l], ffn1_w_down[l])
        h = rms_norm(x, mix_norm[l])
        z = h @ w_in[l]
        qa, ka, va, ub, vb, qc, kc, vc = jnp.split(z, IN_SPLITS, axis=-1)
        heads = lambda t, n: t.reshape(b, s, n, HEAD_DIM)
        qa = rms_norm(heads(qa, NA_HEADS), na_q_norm[l])
        ka = rms_norm(heads(ka, NA_HEADS), na_k_norm[l])
        out_a = neighbourhood_attention(qa, ka, heads(va, NA_HEADS), na_rpb[l])
        out_b = spatial_gating(ub, vb, sg_v_norm[l], sg_w[l], sg_b[l])
        qc = apply_rope(rms_norm(heads(qc, GQA_Q_HEADS), gqa_q_norm[l]), cos, sin)
        kc = apply_rope(rms_norm(heads(kc, GQA_KV_HEADS), gqa_k_norm[l]), cos, sin)
        out_c = gqa_attention(qc, kc, heads(vc, GQA_KV_HEADS))
        x = x + jnp.concatenate([out_a, out_b, out_c], axis=-1) @ w_out[l]
        x = x + 0.5 * swiglu_ffn(rms_norm(x, ffn2_norm[l]), ffn2_w_gate[l], ffn2_w_up[l], ffn2_w_down[l])
    return x


def setup_inputs(seed: int = 0) -> dict:
    key = jax.random.key(seed)
    ks = jax.random.split(key, 24)
    f32 = jnp.float32
    nrm = lambda k, shape, sc: jax.random.normal(k, shape, f32) * sc
    gain = lambda k, shape: 1.0 + 0.01 * jax.random.normal(k, shape, f32)
    L = DEPTH
    return {
        "x_prompt": jax.random.normal(ks[0], (BATCH, SEQ, D_MODEL), f32),
        "x_sample": jax.random.normal(ks[1], (DEC_BATCH, DEC_SEQ, D_MODEL), f32),
        "ffn1_norm": gain(ks[2], (L, D_MODEL)),
        "ffn1_w_gate": nrm(ks[3], (L, D_MODEL, D_FF), D_MODEL ** -0.5),
        "ffn1_w_up": nrm(ks[4], (L, D_MODEL, D_FF), D_MODEL ** -0.5),
        "ffn1_w_down": nrm(ks[5], (L, D_FF, D_MODEL), D_FF ** -0.5),
        "mix_norm": gain(ks[6], (L, D_MODEL)),
        "w_in": nrm(ks[7], (L, D_MODEL, D_IN), D_MODEL ** -0.5),
        "na_q_norm": gain(ks[8], (L, HEAD_DIM)),
        "na_k_norm": gain(ks[9], (L, HEAD_DIM)),
        "na_rpb": nrm(ks[10], (L, NA_HEADS, 2 * WIN_R - 1, 2 * WIN_C - 1), 0.1),
        "sg_v_norm": gain(ks[11], (L, SG_GROUPS, SG_GROUP_DIM)),
        "sg_w": nrm(ks[12], (L, SG_GROUPS, SG_CHUNK, SG_CHUNK), SG_CHUNK ** -0.5),
        "sg_b": 1.0 + nrm(ks[13], (L, SG_GROUPS, SG_CHUNK), 0.02),
        "gqa_q_norm": gain(ks[14], (L, HEAD_DIM)),
        "gqa_k_norm": gain(ks[15], (L, HEAD_DIM)),
        "w_out": nrm(ks[16], (L, D_MIX, D_MODEL), D_MIX ** -0.5),
        "ffn2_norm": gain(ks[17], (L, D_MODEL)),
        "ffn2_w_gate": nrm(ks[18], (L, D_MODEL, D_FF), D_MODEL ** -0.5),
        "ffn2_w_up": nrm(ks[19], (L, D_MODEL, D_FF), D_MODEL ** -0.5),
        "ffn2_w_down": nrm(ks[20], (L, D_FF, D_MODEL), D_FF ** -0.5),
    }


def reference(x_prompt, x_sample, ffn1_norm, ffn1_w_gate, ffn1_w_up, ffn1_w_down, mix_norm, w_in,
              na_q_norm, na_k_norm, na_rpb, sg_v_norm, sg_w, sg_b, gqa_q_norm, gqa_k_norm,
              w_out, ffn2_norm, ffn2_w_gate, ffn2_w_up, ffn2_w_down):
    y_prompt = trunk(x_prompt, ffn1_norm, ffn1_w_gate, ffn1_w_up, ffn1_w_down, mix_norm, w_in,
                     na_q_norm, na_k_norm, na_rpb, sg_v_norm, sg_w, sg_b, gqa_q_norm, gqa_k_norm,
                     w_out, ffn2_norm, ffn2_w_gate, ffn2_w_up, ffn2_w_down)
    y_sample = trunk(x_sample, ffn1_norm, ffn1_w_gate, ffn1_w_up, ffn1_w_down, mix_norm, w_in,
                     na_q_norm, na_k_norm, na_rpb, sg_v_norm, sg_w, sg_b, gqa_q_norm, gqa_k_norm,
                     w_out, ffn2_norm, ffn2_w_gate, ffn2_w_up, ffn2_w_down)
    return (y_prompt, y_sample)
```

```python
import functools

import numpy as np
import jax
import jax.numpy as jnp
from jax import lax
from jax.experimental import pallas as pl
from jax.experimental.pallas import tpu as pltpu

F32 = jnp.float32
BF16 = jnp.bfloat16

D_MODEL = 1024
HEAD_DIM = 64
GRID_W = 64
NA_HEADS = 4
WIN_R = 8
WIN_C = 16
SG_GROUPS = 4
SG_CHUNK = 128
GQA_Q_HEADS = 8
GQA_KV_HEADS = 2
GQA_REP = GQA_Q_HEADS // GQA_KV_HEADS
ROPE_THETA = 10000.0
D_FF = 2816
EPS = 1e-6
LOG2E = 1.4426950408889634
NEG_BIG = -1e30

NA_W = NA_HEADS * HEAD_DIM
SG_W = SG_GROUPS * HEAD_DIM
GQ_W = GQA_Q_HEADS * HEAD_DIM
GKV_W = GQA_KV_HEADS * HEAD_DIM
D_IN = 3 * NA_W + 2 * SG_W + GQ_W + 2 * GKV_W
NA_KEYS = WIN_R * GRID_W

LANES = 128
VMEM_LIMIT_BYTES = 56 * 1024 * 1024

TOKEN_TILE = 256
NA_ROWS_PER_STEP = 8
GQA_TQ = 256
GQA_TK = 512
GQA_CHUNK = 256
ONES_ROWS = 16


def _const_spec(shape, index_map):
    return pl.BlockSpec(shape, index_map, pipeline_mode=pl.Buffered(1))


def _rms(x, g):
    ms = jnp.mean(x * x, axis=-1, keepdims=True)
    return x * lax.rsqrt(ms + EPS) * g


def _gelu(x):
    return 0.5 * x * (1.0 + jnp.tanh(0.7978845608028654 * (x + 0.044715 * (x * x * x))))


def _ffn(x, g_ref, wg_ref, wu_ref, wd_ref):
    xn = _rms(x, g_ref[...]).astype(BF16)
    gate = jnp.dot(xn, wg_ref[...], preferred_element_type=F32)
    up = jnp.dot(xn, wu_ref[...], preferred_element_type=F32)
    h = (gate / (1.0 + jnp.exp(-gate)) * up).astype(BF16)
    y = jnp.dot(h, wd_ref[...], preferred_element_type=F32)
    return x + 0.5 * y


def _group_mean_sq(z, gmat):
    sq = z * z
    hi = sq.astype(BF16)
    lo = (sq - hi.astype(F32)).astype(BF16)
    return (jnp.dot(hi, gmat, preferred_element_type=F32)
            + jnp.dot(lo, gmat, preferred_element_type=F32))


def _head_norm(z, gain, gmat):
    return z * lax.rsqrt(_group_mean_sq(z, gmat) + EPS) * gain


def _rope(x, cos, sin, first_half):
    rot = jnp.where(first_half, pltpu.roll(x, LANES - HEAD_DIM // 2, 1), pltpu.roll(x, HEAD_DIM // 2, 1))
    return x * cos + rot * sin


def _ffn_proj_kernel(x_ref, g1_ref, wg_ref, wu_ref, wd_ref, gm_ref, win_ref, gmat_ref,
                     cos_ref, sin_ref, naq_ref, nak_ref, sgv_ref, gq_ref, gk_ref, sgw_ref, sgb_ref,
                     xo_ref, qa_ref, ka_ref, va_ref, ob_ref, qc_ref, kc_ref, vct_ref):
    tm = x_ref.shape[0]
    x = _ffn(x_ref[...], g1_ref, wg_ref, wu_ref, wd_ref)
    xo_ref[...] = x
    h = _rms(x, gm_ref[...]).astype(BF16)
    z = jnp.dot(h, win_ref[...], preferred_element_type=F32)
    gmat = gmat_ref[...]
    gmat_half = gmat_ref[0:LANES, 0:LANES]

    c0 = 0
    qa = _head_norm(z[:, c0:c0 + NA_W], naq_ref[...], gmat)
    qa_ref[...] = (qa * (HEAD_DIM ** -0.5 * LOG2E)).astype(BF16)
    c0 += NA_W
    ka_ref[...] = _head_norm(z[:, c0:c0 + NA_W], nak_ref[...], gmat).astype(BF16)
    c0 += NA_W
    va_ref[...] = z[:, c0:c0 + NA_W].astype(BF16)
    c0 += NA_W
    u = _gelu(z[:, c0:c0 + SG_W])
    c0 += SG_W
    vn = _head_norm(_gelu(z[:, c0:c0 + SG_W]), sgv_ref[...], gmat)
    c0 += SG_W

    lane = lax.broadcasted_iota(jnp.int32, (tm, LANES), 1)
    first_half = (lane % HEAD_DIM) < (HEAD_DIM // 2)
    cos = cos_ref[...]
    sin = sin_ref[...]
    for s in range(GQ_W // LANES):
        zs = z[:, c0 + s * LANES:c0 + (s + 1) * LANES]
        qn = _head_norm(zs, gq_ref[...], gmat_half)
        qr = _rope(qn, cos, sin, first_half) * (HEAD_DIM ** -0.5 * LOG2E)
        qc_ref[:, s * LANES:(s + 1) * LANES] = qr.astype(BF16)
    c0 += GQ_W
    kn = _head_norm(z[:, c0:c0 + GKV_W], gk_ref[...], gmat_half)
    kc_ref[...] = _rope(kn, cos, sin, first_half).astype(BF16)
    c0 += GKV_W
    vct_ref[...] = z[:, c0:c0 + GKV_W].T.astype(BF16)

    lane_c = lax.broadcasted_iota(jnp.int32, (SG_CHUNK, LANES), 1)
    lo = lane_c < HEAD_DIM
    for ch in range(tm // SG_CHUNK):
        rows = slice(ch * SG_CHUNK, (ch + 1) * SG_CHUNK)
        mixed = []
        for pair in range(SG_GROUPS // 2):
            v128 = vn[rows, pair * LANES:(pair + 1) * LANES]
            rhs = jnp.concatenate([jnp.where(lo, v128, 0.0), jnp.where(lo, 0.0, v128)], axis=0)
            mixed.append(jnp.dot(sgw_ref[pair], rhs.astype(BF16), preferred_element_type=F32))
        mixed = jnp.concatenate(mixed, axis=-1) + sgb_ref[...]
        ob_ref[rows, :] = (u[rows, :] * mixed).astype(BF16)


def _ffn_proj(x, l, p, seq):
    n = x.shape[0]
    tm = TOKEN_TILE
    nt = seq // tm
    row = lambda i: (i, 0)
    const2 = lambda i: (0, 0)
    lsel3 = lambda i: (l, 0, 0)
    in_specs = [
        pl.BlockSpec((tm, D_MODEL), row),
        _const_spec((None, 1, D_MODEL), lsel3),
        _const_spec((None, D_MODEL, D_FF), lsel3),
        _const_spec((None, D_MODEL, D_FF), lsel3),
        _const_spec((None, D_FF, D_MODEL), lsel3),
        _const_spec((None, 1, D_MODEL), lsel3),
        _const_spec((None, D_MODEL, D_IN), lsel3),
        _const_spec((2 * LANES, 2 * LANES), const2),
        pl.BlockSpec((tm, LANES), lambda i: (i % nt, 0)),
        pl.BlockSpec((tm, LANES), lambda i: (i % nt, 0)),
        _const_spec((None, 1, NA_W), lsel3),
        _const_spec((None, 1, NA_W), lsel3),
        _const_spec((None, 1, SG_W), lsel3),
        _const_spec((None, 1, LANES), lsel3),
        _const_spec((None, 1, LANES), lsel3),
        _const_spec((None, SG_GROUPS // 2, SG_CHUNK, 2 * SG_CHUNK), lambda i: (l, 0, 0, 0)),
        _const_spec((None, SG_CHUNK, SG_W), lsel3),
    ]
    out_shape = [
        jax.ShapeDtypeStruct((n, D_MODEL), F32),
        jax.ShapeDtypeStruct((n, NA_W), BF16),
        jax.ShapeDtypeStruct((n, NA_W), BF16),
        jax.ShapeDtypeStruct((n, NA_W), BF16),
        jax.ShapeDtypeStruct((n, SG_W), BF16),
        jax.ShapeDtypeStruct((n, GQ_W), BF16),
        jax.ShapeDtypeStruct((n, GKV_W), BF16),
        jax.ShapeDtypeStruct((GKV_W, n), BF16),
    ]
    out_specs = [
        pl.BlockSpec((tm, D_MODEL), row),
        pl.BlockSpec((tm, NA_W), row),
        pl.BlockSpec((tm, NA_W), row),
        pl.BlockSpec((tm, NA_W), row),
        pl.BlockSpec((tm, SG_W), row),
        pl.BlockSpec((tm, GQ_W), row),
        pl.BlockSpec((tm, GKV_W), row),
        pl.BlockSpec((GKV_W, tm), lambda i: (0, i)),
    ]
    return pl.pallas_call(
        _ffn_proj_kernel,
        grid=(n // tm,),
        in_specs=in_specs,
        out_specs=out_specs,
        out_shape=out_shape,
        compiler_params=pltpu.CompilerParams(
            dimension_semantics=("parallel",), vmem_limit_bytes=VMEM_LIMIT_BYTES),
        name="ffn_proj",
    )(x, p["ffn1_norm"], p["ffn1_w_gate"], p["ffn1_w_up"], p["ffn1_w_down"], p["mix_norm"], p["w_in"],
      p["gmat"], p["cos"], p["sin"], p["na_q_gain"], p["na_k_gain"], p["sg_v_gain"], p["gqa_q_gain"],
      p["gqa_k_gain"], p["sg_w_pairs"], p["sg_bias"])


def _na_kernel(q_ref, k_ref, v_ref, bias_ref, o_ref, *, rows_per_step, grid_rows):
    i = pl.program_id(1)
    lane = lax.broadcasted_iota(jnp.int32, (GRID_W, LANES), 1)
    lo = lane < HEAD_DIM

    def row_body(rr, carry):
        r = i * rows_per_step + rr
        rs = jnp.clip(r - WIN_R // 2, 0, grid_rows - WIN_R)
        t = r - rs
        q0 = pl.multiple_of(rr * GRID_W, GRID_W)
        k0 = pl.multiple_of(rs * GRID_W, GRID_W)
        q_row = q_ref[pl.ds(q0, GRID_W), :]
        kw = k_ref[pl.ds(k0, NA_KEYS), :]
        vw = v_ref[pl.ds(k0, NA_KEYS), :]
        outs = []
        for pair in range(NA_HEADS // 2):
            cols = slice(pair * LANES, (pair + 1) * LANES)
            q2 = q_row[:, cols]
            k2 = kw[:, cols]
            v2 = vw[:, cols]
            o_pair = None
            for hh in range(2):
                qm = jnp.where(lo if hh == 0 else jnp.logical_not(lo), q2, jnp.zeros_like(q2))
                s = lax.dot_general(qm, k2, (((1,), (1,)), ((), ())), preferred_element_type=F32)
                s = s + bias_ref[t, 2 * pair + hh]
                m = jnp.max(s, axis=-1, keepdims=True)
                e = jnp.exp2(s - m)
                denom = jnp.sum(e, axis=-1, keepdims=True)
                pv = jnp.dot(e.astype(BF16), v2, preferred_element_type=F32)
                pv = pv / denom
                o_pair = pv if hh == 0 else jnp.where(lo, o_pair, pv)
            outs.append(o_pair)
        o_ref[pl.ds(q0, GRID_W), :] = jnp.concatenate(outs, axis=-1).astype(BF16)
        return carry

    lax.fori_loop(0, rows_per_step, row_body, 0)


def _na(qa, ka, va, bias_l, batch, seq):
    n = qa.shape[0]
    grid_rows = seq // GRID_W
    rps = NA_ROWS_PER_STEP
    steps = grid_rows // rps
    blk = rps * GRID_W
    return pl.pallas_call(
        functools.partial(_na_kernel, rows_per_step=rps, grid_rows=grid_rows),
        grid=(batch, steps),
        in_specs=[
            pl.BlockSpec((blk, NA_W), lambda b, i: (b * steps + i, 0)),
            pl.BlockSpec((seq, NA_W), lambda b, i: (b, 0)),
            pl.BlockSpec((seq, NA_W), lambda b, i: (b, 0)),
            _const_spec((WIN_R, NA_HEADS, GRID_W, NA_KEYS), lambda b, i: (0, 0, 0, 0)),
        ],
        out_specs=pl.BlockSpec((blk, NA_W), lambda b, i: (b * steps + i, 0)),
        out_shape=jax.ShapeDtypeStruct((n, NA_W), BF16),
        compiler_params=pltpu.CompilerParams(
            dimension_semantics=("parallel", "parallel"), vmem_limit_bytes=VMEM_LIMIT_BYTES),
        name="na",
    )(qa, ka, va, bias_l)


def _na_bias_table(rpb):
    c = np.arange(GRID_W)
    kc = np.arange(GRID_W)
    cs = np.clip(c - WIN_C // 2, 0, GRID_W - WIN_C)
    valid = (kc[None, :] >= cs[:, None]) & (kc[None, :] < cs[:, None] + WIN_C)
    dc = np.clip(kc[None, :] - c[:, None] + (WIN_C - 1), 0, 2 * WIN_C - 2)
    t = np.arange(WIN_R)
    i = np.arange(WIN_R)
    dr = i[None, :] - t[:, None] + (WIN_R - 1)
    tbl = rpb[:, dr][:, :, :, dc]
    tbl = jnp.transpose(tbl, (1, 0, 3, 2, 4)) * LOG2E
    tbl = jnp.where(valid[None, None, :, None, :], tbl, NEG_BIG)
    return tbl.reshape(WIN_R, NA_HEADS, GRID_W, NA_KEYS).astype(F32)


def _gqa_kernel(q_ref, k_ref, vt_ref, o_ref, qt_scr, m_scr, acc_scr, *, tq, tk, seq):
    g = pl.program_id(1)
    m_lanes = GQA_REP * tq

    qt = q_ref[...].astype(F32).T
    qt = jnp.concatenate([qt[h * HEAD_DIM:(h + 1) * HEAD_DIM, :] for h in range(GQA_REP)], axis=1)
    zero = jnp.zeros_like(qt)
    qt_scr[...] = jnp.concatenate([jnp.where(g == 0, qt, zero), jnp.where(g == 1, qt, zero)],
                                  axis=0).astype(BF16)
    m_scr[...] = jnp.full(m_scr.shape, NEG_BIG, F32)
    acc_scr[...] = jnp.zeros(acc_scr.shape, F32)
    ones = jnp.ones((ONES_ROWS, tk), BF16)
    v_row0 = pl.multiple_of(g * HEAD_DIM, HEAD_DIM)

    def kv_step(j, carry):
        k0 = pl.multiple_of(j * tk, tk)
        kj = k_ref[pl.ds(k0, tk), :]
        vext = jnp.concatenate([vt_ref[pl.ds(v_row0, HEAD_DIM), pl.ds(k0, tk)], ones], axis=0)
        for c in range(m_lanes // GQA_CHUNK):
            cols = slice(c * GQA_CHUNK, (c + 1) * GQA_CHUNK)
            st = jnp.dot(kj, qt_scr[:, cols], preferred_element_type=F32)
            m_old = m_scr[:, cols]
            m_new = jnp.maximum(m_old, jnp.max(st, axis=0, keepdims=True))
            alpha = jnp.exp2(m_old - m_new)
            pt = jnp.exp2(st - m_new).astype(BF16)
            acc_scr[:, cols] = acc_scr[:, cols] * alpha + jnp.dot(vext, pt, preferred_element_type=F32)
            m_scr[:, cols] = m_new
        return carry

    lax.fori_loop(0, seq // tk, kv_step, 0)

    acc = acc_scr[...]
    ot = acc[0:HEAD_DIM, :] / acc[HEAD_DIM:HEAD_DIM + 1, :]
    o = jnp.concatenate([ot[:, h * tq:(h + 1) * tq].T for h in range(GQA_REP)], axis=1)
    o_ref[...] = o.astype(BF16)


def _gqa(qc, kc, vct, batch, seq):
    n = qc.shape[0]
    tq, tk = GQA_TQ, GQA_TK
    nq = seq // tq
    qw = GQA_REP * HEAD_DIM
    return pl.pallas_call(
        functools.partial(_gqa_kernel, tq=tq, tk=tk, seq=seq),
        grid=(batch, GQA_KV_HEADS, nq),
        in_specs=[
            pl.BlockSpec((tq, qw), lambda b, g, i: (b * nq + i, g)),
            pl.BlockSpec((seq, GKV_W), lambda b, g, i: (b, 0)),
            pl.BlockSpec((GKV_W, seq), lambda b, g, i: (0, b)),
        ],
        out_specs=pl.BlockSpec((tq, qw), lambda b, g, i: (b * nq + i, g)),
        out_shape=jax.ShapeDtypeStruct((n, GQ_W), BF16),
        scratch_shapes=[
            pltpu.VMEM((GKV_W, GQA_REP * tq), BF16),
            pltpu.VMEM((1, GQA_REP * tq), F32),
            pltpu.VMEM((HEAD_DIM + ONES_ROWS, GQA_REP * tq), F32),
        ],
        compiler_params=pltpu.CompilerParams(
            dimension_semantics=("parallel", "parallel", "parallel"), vmem_limit_bytes=VMEM_LIMIT_BYTES),
        name="gqa",
    )(qc, kc, vct)


def _out_ffn_kernel(x_ref, a_ref, b_ref, c_ref, wo_ref, g2_ref, wg_ref, wu_ref, wd_ref, o_ref):
    mix = jnp.concatenate([a_ref[...], b_ref[...], c_ref[...]], axis=-1)
    x = x_ref[...] + jnp.dot(mix, wo_ref[...], preferred_element_type=F32)
    o_ref[...] = _ffn(x, g2_ref, wg_ref, wu_ref, wd_ref)


def _out_ffn(x, oa, ob, oc, l, p):
    n = x.shape[0]
    tm = TOKEN_TILE
    row = lambda i: (i, 0)
    lsel3 = lambda i: (l, 0, 0)
    return pl.pallas_call(
        _out_ffn_kernel,
        grid=(n // tm,),
        in_specs=[
            pl.BlockSpec((tm, D_MODEL), row),
            pl.BlockSpec((tm, NA_W), row),
            pl.BlockSpec((tm, SG_W), row),
            pl.BlockSpec((tm, GQ_W), row),
            _const_spec((None, D_MODEL, D_MODEL), lsel3),
            _const_spec((None, 1, D_MODEL), lsel3),
            _const_spec((None, D_MODEL, D_FF), lsel3),
            _const_spec((None, D_MODEL, D_FF), lsel3),
            _const_spec((None, D_FF, D_MODEL), lsel3),
        ],
        out_specs=pl.BlockSpec((tm, D_MODEL), row),
        out_shape=jax.ShapeDtypeStruct((n, D_MODEL), F32),
        compiler_params=pltpu.CompilerParams(
            dimension_semantics=("parallel",), vmem_limit_bytes=VMEM_LIMIT_BYTES),
        name="out_ffn",
    )(x, oa, ob, oc, p["w_out"], p["ffn2_norm"], p["ffn2_w_gate"], p["ffn2_w_up"], p["ffn2_w_down"])


def _rope_tables(seq):
    t = np.arange(seq)
    row = (t // GRID_W).astype(np.float32)
    col = (t % GRID_W).astype(np.float32)
    n_freq = HEAD_DIM // 4
    inv = jnp.asarray(ROPE_THETA, F32) ** (-jnp.arange(n_freq, dtype=F32) / n_freq)
    ang = jnp.concatenate([jnp.asarray(row)[:, None] * inv, jnp.asarray(col)[:, None] * inv], axis=-1)
    cos, sin = jnp.cos(ang), jnp.sin(ang)
    reps = LANES // HEAD_DIM
    cos_t = jnp.tile(jnp.concatenate([cos, cos], axis=-1), (1, reps))
    sin_t = jnp.tile(jnp.concatenate([-sin, sin], axis=-1), (1, reps))
    return cos_t, sin_t


def _prepare(ffn1_norm, ffn1_w_gate, ffn1_w_up, ffn1_w_down, mix_norm, w_in, na_q_norm, na_k_norm,
             na_rpb, sg_v_norm, sg_w, sg_b, gqa_q_norm, gqa_k_norm, w_out, ffn2_norm, ffn2_w_gate,
             ffn2_w_up, ffn2_w_down, seq):
    depth = w_in.shape[0]
    cos_t, sin_t = _rope_tables(seq)
    gm = np.kron(np.eye(2 * LANES // HEAD_DIM), np.ones((HEAD_DIM, HEAD_DIM))) / HEAD_DIM
    sg_pairs = jnp.concatenate([sg_w[:, 0::2], sg_w[:, 1::2]], axis=-1).astype(BF16)
    sg_bias = jnp.repeat(jnp.transpose(sg_b, (0, 2, 1)), HEAD_DIM, axis=-1)
    vec = lambda a, reps=1: jnp.tile(a.reshape(depth, 1, -1), (1, 1, reps))
    return dict(
        ffn1_norm=vec(ffn1_norm), ffn1_w_gate=ffn1_w_gate.astype(BF16), ffn1_w_up=ffn1_w_up.astype(BF16),
        ffn1_w_down=ffn1_w_down.astype(BF16), mix_norm=vec(mix_norm), w_in=w_in.astype(BF16),
        gmat=jnp.asarray(gm, BF16), cos=cos_t, sin=sin_t,
        na_q_gain=vec(na_q_norm, NA_HEADS), na_k_gain=vec(na_k_norm, NA_HEADS),
        sg_v_gain=vec(sg_v_norm),
        gqa_q_gain=vec(gqa_q_norm, LANES // HEAD_DIM), gqa_k_gain=vec(gqa_k_norm, LANES // HEAD_DIM),
        sg_w_pairs=sg_pairs, sg_bias=sg_bias,
        na_bias=jax.vmap(_na_bias_table)(na_rpb),
        w_out=w_out.astype(BF16), ffn2_norm=vec(ffn2_norm), ffn2_w_gate=ffn2_w_gate.astype(BF16),
        ffn2_w_up=ffn2_w_up.astype(BF16), ffn2_w_down=ffn2_w_down.astype(BF16),
    )


def _trunk(x, p, depth):
    batch, seq, _ = x.shape
    x = x.reshape(batch * seq, D_MODEL)
    for l in range(depth):
        x, qa, ka, va, ob, qc, kc, vct = _ffn_proj(x, l, p, seq)
        oa = _na(qa, ka, va, p["na_bias"][l], batch, seq)
        oc = _gqa(qc, kc, vct, batch, seq)
        x = _out_ffn(x, oa, ob, oc, l, p)
    return x.reshape(batch, seq, D_MODEL)


def kernel(x_prompt, x_sample, ffn1_norm, ffn1_w_gate, ffn1_w_up, ffn1_w_down, mix_norm, w_in, na_q_norm, na_k_norm, na_rpb, sg_v_norm, sg_w, sg_b, gqa_q_norm, gqa_k_norm, w_out, ffn2_norm, ffn2_w_gate, ffn2_w_up, ffn2_w_down):
    seq = x_prompt.shape[1]
    assert x_sample.shape[1] == seq and seq % (GRID_W * NA_ROWS_PER_STEP) == 0
    p = _prepare(ffn1_norm, ffn1_w_gate, ffn1_w_up, ffn1_w_down, mix_norm, w_in, na_q_norm, na_k_norm,
                 na_rpb, sg_v_norm, sg_w, sg_b, gqa_q_norm, gqa_k_norm, w_out, ffn2_norm, ffn2_w_gate,
                 ffn2_w_up, ffn2_w_down, seq)
    n_prompt = x_prompt.shape[0]
    y = _trunk(jnp.concatenate([x_prompt, x_sample], axis=0), p, w_in.shape[0])
    return (y[:n_prompt], y[n_prompt:])
```

```python
import functools

import numpy as np
import jax
import jax.numpy as jnp
from jax import lax
from jax.experimental import pallas as pl
from jax.experimental.pallas import tpu as pltpu

F32 = jnp.float32
BF16 = jnp.bfloat16

D_MODEL = 1024
HEAD_DIM = 64
GRID_W = 64
NA_HEADS = 4
WIN_R = 8
WIN_C = 16
SG_GROUPS = 4
SG_CHUNK = 128
GQA_Q_HEADS = 8
GQA_KV_HEADS = 2
GQA_REP = GQA_Q_HEADS // GQA_KV_HEADS
ROPE_THETA = 10000.0
D_FF = 2816
EPS = 1e-6
LOG2E = 1.4426950408889634
NEG_BIG = -1e30

NA_W = NA_HEADS * HEAD_DIM
SG_W = SG_GROUPS * HEAD_DIM
GQ_W = GQA_Q_HEADS * HEAD_DIM
GKV_W = GQA_KV_HEADS * HEAD_DIM
D_IN = 3 * NA_W + 2 * SG_W + GQ_W + 2 * GKV_W
NA_KEYS = WIN_R * GRID_W

LANES = 128
VMEM_LIMIT_BYTES = 56 * 1024 * 1024

TOKEN_TILE = 256
NA_ROWS_PER_STEP = 8
GQA_TQ = 256
GQA_TK = 512
GQA_CHUNK = 256
GQA_UNROLL = 8
ONES_ROWS = 16


def _const_spec(shape, index_map):
    return pl.BlockSpec(shape, index_map, pipeline_mode=pl.Buffered(1))


def _rms(x, g):
    ms = jnp.mean(x * x, axis=-1, keepdims=True)
    return x * lax.rsqrt(ms + EPS) * g


def _gelu(x):
    return 0.5 * x * (1.0 + jnp.tanh(0.7978845608028654 * (x + 0.044715 * (x * x * x))))


def _ffn(x, g_ref, wg_ref, wu_ref, wd_ref):
    xn = _rms(x, g_ref[...]).astype(BF16)
    gate = jnp.dot(xn, wg_ref[...], preferred_element_type=F32)
    up = jnp.dot(xn, wu_ref[...], preferred_element_type=F32)
    h = (gate / (1.0 + jnp.exp(-gate)) * up).astype(BF16)
    y = jnp.dot(h, wd_ref[...], preferred_element_type=F32)
    return x + 0.5 * y


def _group_mean_sq(z, gmat):
    sq = z * z
    hi = sq.astype(BF16)
    lo = (sq - hi.astype(F32)).astype(BF16)
    return (jnp.dot(hi, gmat, preferred_element_type=F32)
            + jnp.dot(lo, gmat, preferred_element_type=F32))


def _head_norm(z, gain, gmat):
    return z * lax.rsqrt(_group_mean_sq(z, gmat) + EPS) * gain


def _rope(x, cos, sin, first_half):
    rot = jnp.where(first_half, pltpu.roll(x, LANES - HEAD_DIM // 2, 1), pltpu.roll(x, HEAD_DIM // 2, 1))
    return x * cos + rot * sin


def _ffn_proj_kernel(x_ref, g1_ref, wg_ref, wu_ref, wd_ref, gm_ref, win_ref, gmat_ref,
                     cos_ref, sin_ref, naq_ref, nak_ref, sgv_ref, gq_ref, gk_ref, sgw_ref, sgb_ref,
                     xo_ref, qa_ref, ka_ref, va_ref, ob_ref, qc_ref, kc_ref, vct_ref):
    tm = x_ref.shape[0]
    x = _ffn(x_ref[...], g1_ref, wg_ref, wu_ref, wd_ref)
    xo_ref[...] = x
    h = _rms(x, gm_ref[...]).astype(BF16)
    z = jnp.dot(h, win_ref[...], preferred_element_type=F32)
    gmat = gmat_ref[...]
    gmat_half = gmat_ref[0:LANES, 0:LANES]

    c0 = 0
    qa = _head_norm(z[:, c0:c0 + NA_W], naq_ref[...], gmat)
    qa_ref[...] = (qa * (HEAD_DIM ** -0.5 * LOG2E)).astype(BF16)
    c0 += NA_W
    ka_ref[...] = _head_norm(z[:, c0:c0 + NA_W], nak_ref[...], gmat).astype(BF16)
    c0 += NA_W
    va_ref[...] = z[:, c0:c0 + NA_W].astype(BF16)
    c0 += NA_W
    u = _gelu(z[:, c0:c0 + SG_W])
    c0 += SG_W
    vn = _head_norm(_gelu(z[:, c0:c0 + SG_W]), sgv_ref[...], gmat)
    c0 += SG_W

    lane = lax.broadcasted_iota(jnp.int32, (tm, LANES), 1)
    first_half = (lane % HEAD_DIM) < (HEAD_DIM // 2)
    cos = cos_ref[...]
    sin = sin_ref[...]
    for s in range(GQ_W // LANES):
        zs = z[:, c0 + s * LANES:c0 + (s + 1) * LANES]
        qn = _head_norm(zs, gq_ref[...], gmat_half)
        qr = _rope(qn, cos, sin, first_half) * (HEAD_DIM ** -0.5 * LOG2E)
        qc_ref[:, s * LANES:(s + 1) * LANES] = qr.astype(BF16)
    c0 += GQ_W
    kn = _head_norm(z[:, c0:c0 + GKV_W], gk_ref[...], gmat_half)
    kc_ref[...] = _rope(kn, cos, sin, first_half).astype(BF16)
    c0 += GKV_W
    vct_ref[...] = z[:, c0:c0 + GKV_W].T.astype(BF16)

    lane_c = lax.broadcasted_iota(jnp.int32, (SG_CHUNK, LANES), 1)
    lo = lane_c < HEAD_DIM
    for ch in range(tm // SG_CHUNK):
        rows = slice(ch * SG_CHUNK, (ch + 1) * SG_CHUNK)
        mixed = []
        for pair in range(SG_GROUPS // 2):
            v128 = vn[rows, pair * LANES:(pair + 1) * LANES]
            rhs = jnp.concatenate([jnp.where(lo, v128, 0.0), jnp.where(lo, 0.0, v128)], axis=0)
            mixed.append(jnp.dot(sgw_ref[pair], rhs.astype(BF16), preferred_element_type=F32))
        mixed = jnp.concatenate(mixed, axis=-1) + sgb_ref[...]
        ob_ref[rows, :] = (u[rows, :] * mixed).astype(BF16)


def _ffn_proj(x, l, p, seq):
    n = x.shape[0]
    tm = TOKEN_TILE
    nt = seq // tm
    row = lambda i: (i, 0)
    const2 = lambda i: (0, 0)
    lsel3 = lambda i: (l, 0, 0)
    in_specs = [
        pl.BlockSpec((tm, D_MODEL), row),
        _const_spec((None, 1, D_MODEL), lsel3),
        _const_spec((None, D_MODEL, D_FF), lsel3),
        _const_spec((None, D_MODEL, D_FF), lsel3),
        _const_spec((None, D_FF, D_MODEL), lsel3),
        _const_spec((None, 1, D_MODEL), lsel3),
        _const_spec((None, D_MODEL, D_IN), lsel3),
        _const_spec((2 * LANES, 2 * LANES), const2),
        pl.BlockSpec((tm, LANES), lambda i: (i % nt, 0)),
        pl.BlockSpec((tm, LANES), lambda i: (i % nt, 0)),
        _const_spec((None, 1, NA_W), lsel3),
        _const_spec((None, 1, NA_W), lsel3),
        _const_spec((None, 1, SG_W), lsel3),
        _const_spec((None, 1, LANES), lsel3),
        _const_spec((None, 1, LANES), lsel3),
        _const_spec((None, SG_GROUPS // 2, SG_CHUNK, 2 * SG_CHUNK), lambda i: (l, 0, 0, 0)),
        _const_spec((None, SG_CHUNK, SG_W), lsel3),
    ]
    out_shape = [
        jax.ShapeDtypeStruct((n, D_MODEL), F32),
        jax.ShapeDtypeStruct((n, NA_W), BF16),
        jax.ShapeDtypeStruct((n, NA_W), BF16),
        jax.ShapeDtypeStruct((n, NA_W), BF16),
        jax.ShapeDtypeStruct((n, SG_W), BF16),
        jax.ShapeDtypeStruct((n, GQ_W), BF16),
        jax.ShapeDtypeStruct((n, GKV_W), BF16),
        jax.ShapeDtypeStruct((GKV_W, n), BF16),
    ]
    out_specs = [
        pl.BlockSpec((tm, D_MODEL), row),
        pl.BlockSpec((tm, NA_W), row),
        pl.BlockSpec((tm, NA_W), row),
        pl.BlockSpec((tm, NA_W), row),
        pl.BlockSpec((tm, SG_W), row),
        pl.BlockSpec((tm, GQ_W), row),
        pl.BlockSpec((tm, GKV_W), row),
        pl.BlockSpec((GKV_W, tm), lambda i: (0, i)),
    ]
    return pl.pallas_call(
        _ffn_proj_kernel,
        grid=(n // tm,),
        in_specs=in_specs,
        out_specs=out_specs,
        out_shape=out_shape,
        compiler_params=pltpu.CompilerParams(
            dimension_semantics=("parallel",), vmem_limit_bytes=VMEM_LIMIT_BYTES),
        name="ffn_proj",
    )(x, p["ffn1_norm"], p["ffn1_w_gate"], p["ffn1_w_up"], p["ffn1_w_down"], p["mix_norm"], p["w_in"],
      p["gmat"], p["cos"], p["sin"], p["na_q_gain"], p["na_k_gain"], p["sg_v_gain"], p["gqa_q_gain"],
      p["gqa_k_gain"], p["sg_w_pairs"], p["sg_bias"])


def _na_kernel(q_ref, k_ref, v_ref, bias_ref, o_ref, *, rows_per_step, grid_rows):
    i = pl.program_id(1)
    lane_head = lax.broadcasted_iota(jnp.int32, (GRID_W, NA_W), 1) // HEAD_DIM
    head_mask = [lane_head == h for h in range(NA_HEADS)]

    rows = []
    for rr in range(rows_per_step):
        r = i * rows_per_step + rr
        rs = jnp.clip(r - WIN_R // 2, 0, grid_rows - WIN_R)
        rows.append((rr * GRID_W, pl.multiple_of(rs * GRID_W, GRID_W), r - rs))

    scores = []
    for q0, k0, t in rows:
        q_row = q_ref[q0:q0 + GRID_W, :]
        zero = jnp.zeros_like(q_row)
        q_blk = jnp.concatenate([jnp.where(head_mask[h], q_row, zero) for h in range(NA_HEADS)], axis=0)
        kw = k_ref[pl.ds(k0, NA_KEYS), :]
        scores.append(lax.dot_general(q_blk, kw, (((1,), (1,)), ((), ())), preferred_element_type=F32))

    probs = []
    for (q0, k0, t), s in zip(rows, scores):
        s = s + bias_ref[t]
        m = jnp.max(s, axis=-1, keepdims=True)
        e = jnp.exp2(s - m)
        probs.append((e.astype(BF16), jnp.sum(e, axis=-1, keepdims=True)))

    for (q0, k0, t), (e, denom) in zip(rows, probs):
        vw = v_ref[pl.ds(k0, NA_KEYS), :]
        pv = jnp.dot(e, vw, preferred_element_type=F32) / denom
        out = jnp.where(head_mask[0], pv[0:GRID_W], 0.0)
        for h in range(1, NA_HEADS):
            out = jnp.where(head_mask[h], pv[h * GRID_W:(h + 1) * GRID_W], out)
        o_ref[q0:q0 + GRID_W, :] = out.astype(BF16)


def _na(qa, ka, va, bias_l, batch, seq):
    n = qa.shape[0]
    grid_rows = seq // GRID_W
    rps = NA_ROWS_PER_STEP
    steps = grid_rows // rps
    blk = rps * GRID_W
    return pl.pallas_call(
        functools.partial(_na_kernel, rows_per_step=rps, grid_rows=grid_rows),
        grid=(batch, steps),
        in_specs=[
            pl.BlockSpec((blk, NA_W), lambda b, i: (b * steps + i, 0)),
            pl.BlockSpec((seq, NA_W), lambda b, i: (b, 0)),
            pl.BlockSpec((seq, NA_W), lambda b, i: (b, 0)),
            _const_spec((WIN_R, NA_HEADS * GRID_W, NA_KEYS), lambda b, i: (0, 0, 0)),
        ],
        out_specs=pl.BlockSpec((blk, NA_W), lambda b, i: (b * steps + i, 0)),
        out_shape=jax.ShapeDtypeStruct((n, NA_W), BF16),
        compiler_params=pltpu.CompilerParams(
            dimension_semantics=("parallel", "parallel"), vmem_limit_bytes=VMEM_LIMIT_BYTES),
        name="na",
    )(qa, ka, va, bias_l)


def _na_bias_table(rpb):
    c = np.arange(GRID_W)
    kc = np.arange(GRID_W)
    cs = np.clip(c - WIN_C // 2, 0, GRID_W - WIN_C)
    valid = (kc[None, :] >= cs[:, None]) & (kc[None, :] < cs[:, None] + WIN_C)
    dc = np.clip(kc[None, :] - c[:, None] + (WIN_C - 1), 0, 2 * WIN_C - 2)
    t = np.arange(WIN_R)
    i = np.arange(WIN_R)
    dr = i[None, :] - t[:, None] + (WIN_R - 1)
    tbl = rpb[:, dr][:, :, :, dc]
    tbl = jnp.transpose(tbl, (1, 0, 3, 2, 4)) * LOG2E
    tbl = jnp.where(valid[None, None, :, None, :], tbl, NEG_BIG)
    return tbl.reshape(WIN_R, NA_HEADS * GRID_W, NA_KEYS).astype(F32)


def _gqa_kernel(q_ref, k_ref, vt_ref, o_ref, qt_scr, s_scr, mc_scr, m_scr, acc_scr, *, tq, tk, seq):
    g = pl.program_id(1)
    m_lanes = GQA_REP * tq
    n_kv = seq // tk

    qt = q_ref[...].astype(F32).T
    qt = jnp.concatenate([qt[h * HEAD_DIM:(h + 1) * HEAD_DIM, :] for h in range(GQA_REP)], axis=1)
    zero = jnp.zeros_like(qt)
    qt_scr[...] = jnp.concatenate([jnp.where(g == 0, qt, zero), jnp.where(g == 1, qt, zero)],
                                  axis=0).astype(BF16)
    m_scr[...] = jnp.full(m_scr.shape, NEG_BIG, F32)
    acc_scr[...] = jnp.zeros(acc_scr.shape, F32)
    ones = jnp.ones((ONES_ROWS, tk), BF16)
    v_row0 = pl.multiple_of(g * HEAD_DIM, HEAD_DIM)

    chunks = [slice(c * GQA_CHUNK, (c + 1) * GQA_CHUNK) for c in range(m_lanes // GQA_CHUNK)]

    def scores(j, slot):
        k0 = pl.multiple_of(j * tk, tk)
        kj = k_ref[pl.ds(k0, tk), :]
        for cols in chunks:
            st = jnp.dot(kj, qt_scr[:, cols], preferred_element_type=F32)
            s_scr[slot, :, cols] = st
            mc_scr[slot, :, cols] = jnp.max(st, axis=0, keepdims=True)

    def softmax_pv(j, slot):
        k0 = pl.multiple_of(j * tk, tk)
        vext = jnp.concatenate([vt_ref[pl.ds(v_row0, HEAD_DIM), pl.ds(k0, tk)], ones], axis=0)
        for cols in chunks:
            m_old = m_scr[:, cols]
            m_new = jnp.maximum(m_old, mc_scr[slot, :, cols])
            alpha = jnp.exp2(m_old - m_new)
            pt = jnp.exp2(s_scr[slot, :, cols] - m_new).astype(BF16)
            acc_scr[:, cols] = acc_scr[:, cols] * alpha + jnp.dot(vext, pt, preferred_element_type=F32)
            m_scr[:, cols] = m_new

    scores(0, 0)

    def kv_steps(jj, carry):
        for u in range(GQA_UNROLL // 2):
            j0 = GQA_UNROLL * jj + 2 * u
            scores(j0 + 1, 1)
            softmax_pv(j0, 0)
            scores(jnp.minimum(j0 + 2, n_kv - 1), 0)
            softmax_pv(j0 + 1, 1)
        return carry

    lax.fori_loop(0, n_kv // GQA_UNROLL, kv_steps, 0)

    acc = acc_scr[...]
    ot = acc[0:HEAD_DIM, :] / acc[HEAD_DIM:HEAD_DIM + 1, :]
    o = jnp.concatenate([ot[:, h * tq:(h + 1) * tq].T for h in range(GQA_REP)], axis=1)
    o_ref[...] = o.astype(BF16)


def _gqa(qc, kc, vct, batch, seq):
    n = qc.shape[0]
    tq, tk = GQA_TQ, GQA_TK
    assert seq % tq == 0 and seq % (tk * GQA_UNROLL) == 0
    nq = seq // tq
    qw = GQA_REP * HEAD_DIM
    return pl.pallas_call(
        functools.partial(_gqa_kernel, tq=tq, tk=tk, seq=seq),
        grid=(batch, GQA_KV_HEADS, nq),
        in_specs=[
            pl.BlockSpec((tq, qw), lambda b, g, i: (b * nq + i, g)),
            pl.BlockSpec((seq, GKV_W), lambda b, g, i: (b, 0)),
            pl.BlockSpec((GKV_W, seq), lambda b, g, i: (0, b)),
        ],
        out_specs=pl.BlockSpec((tq, qw), lambda b, g, i: (b * nq + i, g)),
        out_shape=jax.ShapeDtypeStruct((n, GQ_W), BF16),
        scratch_shapes=[
            pltpu.VMEM((GKV_W, GQA_REP * tq), BF16),
            pltpu.VMEM((2, tk, GQA_REP * tq), F32),
            pltpu.VMEM((2, 1, GQA_REP * tq), F32),
            pltpu.VMEM((1, GQA_REP * tq), F32),
            pltpu.VMEM((HEAD_DIM + ONES_ROWS, GQA_REP * tq), F32),
        ],
        compiler_params=pltpu.CompilerParams(
            dimension_semantics=("parallel", "parallel", "parallel"), vmem_limit_bytes=VMEM_LIMIT_BYTES),
        name="gqa",
    )(qc, kc, vct)


def _out_ffn_kernel(x_ref, a_ref, b_ref, c_ref, wo_ref, g2_ref, wg_ref, wu_ref, wd_ref, o_ref):
    mix = jnp.concatenate([a_ref[...], b_ref[...], c_ref[...]], axis=-1)
    x = x_ref[...] + jnp.dot(mix, wo_ref[...], preferred_element_type=F32)
    o_ref[...] = _ffn(x, g2_ref, wg_ref, wu_ref, wd_ref)


def _out_ffn(x, oa, ob, oc, l, p):
    n = x.shape[0]
    tm = TOKEN_TILE
    row = lambda i: (i, 0)
    lsel3 = lambda i: (l, 0, 0)
    return pl.pallas_call(
        _out_ffn_kernel,
        grid=(n // tm,),
        in_specs=[
            pl.BlockSpec((tm, D_MODEL), row),
            pl.BlockSpec((tm, NA_W), row),
            pl.BlockSpec((tm, SG_W), row),
            pl.BlockSpec((tm, GQ_W), row),
            _const_spec((None, D_MODEL, D_MODEL), lsel3),
            _const_spec((None, 1, D_MODEL), lsel3),
            _const_spec((None, D_MODEL, D_FF), lsel3),
            _const_spec((None, D_MODEL, D_FF), lsel3),
            _const_spec((None, D_FF, D_MODEL), lsel3),
        ],
        out_specs=pl.BlockSpec((tm, D_MODEL), row),
        out_shape=jax.ShapeDtypeStruct((n, D_MODEL), F32),
        compiler_params=pltpu.CompilerParams(
            dimension_semantics=("parallel",), vmem_limit_bytes=VMEM_LIMIT_BYTES),
        name="out_ffn",
    )(x, oa, ob, oc, p["w_out"], p["ffn2_norm"], p["ffn2_w_gate"], p["ffn2_w_up"], p["ffn2_w_down"])


def _rope_tables(seq):
    t = np.arange(seq)
    row = (t // GRID_W).astype(np.float32)
    col = (t % GRID_W).astype(np.float32)
    n_freq = HEAD_DIM // 4
    inv = jnp.asarray(ROPE_THETA, F32) ** (-jnp.arange(n_freq, dtype=F32) / n_freq)
    ang = jnp.concatenate([jnp.asarray(row)[:, None] * inv, jnp.asarray(col)[:, None] * inv], axis=-1)
    cos, sin = jnp.cos(ang), jnp.sin(ang)
    reps = LANES // HEAD_DIM
    cos_t = jnp.tile(jnp.concatenate([cos, cos], axis=-1), (1, reps))
    sin_t = jnp.tile(jnp.concatenate([-sin, sin], axis=-1), (1, reps))
    return cos_t, sin_t


def _prepare(ffn1_norm, ffn1_w_gate, ffn1_w_up, ffn1_w_down, mix_norm, w_in, na_q_norm, na_k_norm,
             na_rpb, sg_v_norm, sg_w, sg_b, gqa_q_norm, gqa_k_norm, w_out, ffn2_norm, ffn2_w_gate,
             ffn2_w_up, ffn2_w_down, seq):
    depth = w_in.shape[0]
    cos_t, sin_t = _rope_tables(seq)
    gm = np.kron(np.eye(2 * LANES // HEAD_DIM), np.ones((HEAD_DIM, HEAD_DIM))) / HEAD_DIM
    sg_pairs = jnp.concatenate([sg_w[:, 0::2], sg_w[:, 1::2]], axis=-1).astype(BF16)
    sg_bias = jnp.repeat(jnp.transpose(sg_b, (0, 2, 1)), HEAD_DIM, axis=-1)
    vec = lambda a, reps=1: jnp.tile(a.reshape(depth, 1, -1), (1, 1, reps))
    return dict(
        ffn1_norm=vec(ffn1_norm), ffn1_w_gate=ffn1_w_gate.astype(BF16), ffn1_w_up=ffn1_w_up.astype(BF16),
        ffn1_w_down=ffn1_w_down.astype(BF16), mix_norm=vec(mix_norm), w_in=w_in.astype(BF16),
        gmat=jnp.asarray(gm, BF16), cos=cos_t, sin=sin_t,
        na_q_gain=vec(na_q_norm, NA_HEADS), na_k_gain=vec(na_k_norm, NA_HEADS),
        sg_v_gain=vec(sg_v_norm),
        gqa_q_gain=vec(gqa_q_norm, LANES // HEAD_DIM), gqa_k_gain=vec(gqa_k_norm, LANES // HEAD_DIM),
        sg_w_pairs=sg_pairs, sg_bias=sg_bias,
        na_bias=jax.vmap(_na_bias_table)(na_rpb),
        w_out=w_out.astype(BF16), ffn2_norm=vec(ffn2_norm), ffn2_w_gate=ffn2_w_gate.astype(BF16),
        ffn2_w_up=ffn2_w_up.astype(BF16), ffn2_w_down=ffn2_w_down.astype(BF16),
    )


def _trunk(x, p, depth):
    batch, seq, _ = x.shape
    x = x.reshape(batch * seq, D_MODEL)
    for l in range(depth):
        x, qa, ka, va, ob, qc, kc, vct = _ffn_proj(x, l, p, seq)
        oa = _na(qa, ka, va, p["na_bias"][l], batch, seq)
        oc = _gqa(qc, kc, vct, batch, seq)
        x = _out_ffn(x, oa, ob, oc, l, p)
    return x.reshape(batch, seq, D_MODEL)


def kernel(x_prompt, x_sample, ffn1_norm, ffn1_w_gate, ffn1_w_up, ffn1_w_down, mix_norm, w_in, na_q_norm, na_k_norm, na_rpb, sg_v_norm, sg_w, sg_b, gqa_q_norm, gqa_k_norm, w_out, ffn2_norm, ffn2_w_gate, ffn2_w_up, ffn2_w_down):
    seq = x_prompt.shape[1]
    assert x_sample.shape[1] == seq and seq % (GRID_W * NA_ROWS_PER_STEP) == 0
    p = _prepare(ffn1_norm, ffn1_w_gate, ffn1_w_up, ffn1_w_down, mix_norm, w_in, na_q_norm, na_k_norm,
                 na_rpb, sg_v_norm, sg_w, sg_b, gqa_q_norm, gqa_k_norm, w_out, ffn2_norm, ffn2_w_gate,
                 ffn2_w_up, ffn2_w_down, seq)
    n_prompt = x_prompt.shape[0]
    y = _trunk(jnp.concatenate([x_prompt, x_sample], axis=0), p, w_in.shape[0])
    return (y[:n_prompt], y[n_prompt:])
```

```python
import functools

import numpy as np
import jax
import jax.numpy as jnp
from jax import lax
from jax.experimental import pallas as pl
from jax.experimental.pallas import tpu as pltpu

F32 = jnp.float32
BF16 = jnp.bfloat16

D_MODEL = 1024
HEAD_DIM = 64
GRID_W = 64
NA_HEADS = 4
WIN_R = 8
WIN_C = 16
SG_GROUPS = 4
SG_CHUNK = 128
GQA_Q_HEADS = 8
GQA_KV_HEADS = 2
GQA_REP = GQA_Q_HEADS // GQA_KV_HEADS
ROPE_THETA = 10000.0
D_FF = 2816
EPS = 1e-6
LOG2E = 1.4426950408889634
NEG_BIG = -1e30

NA_W = NA_HEADS * HEAD_DIM
SG_W = SG_GROUPS * HEAD_DIM
GQ_W = GQA_Q_HEADS * HEAD_DIM
GKV_W = GQA_KV_HEADS * HEAD_DIM
D_IN = 3 * NA_W + 2 * SG_W + GQ_W + 2 * GKV_W
NA_KEYS = WIN_R * GRID_W

LANES = 128
VMEM_LIMIT_BYTES = 56 * 1024 * 1024

TOKEN_TILE = 512
TOKEN_SUB_TILES = 2
NA_ROWS_PER_STEP = 8
GQA_TQ = 256
GQA_TK = 256
GQA_CHUNK = 256
GQA_UNROLL = 16
ONES_ROWS = 16


def _const_spec(shape, index_map):
    return pl.BlockSpec(shape, index_map, pipeline_mode=pl.Buffered(1))


def _rms(x, g):
    ms = jnp.mean(x * x, axis=-1, keepdims=True)
    return x * lax.rsqrt(ms + EPS) * g


def _gelu(x):
    return 0.5 * x * (1.0 + jnp.tanh(0.7978845608028654 * (x + 0.044715 * (x * x * x))))


def _ffn(xs, g_ref, wg_ref, wu_ref, wd_ref):
    hs = []
    for x in xs:
        xn = _rms(x, g_ref[...]).astype(BF16)
        gate = jnp.dot(xn, wg_ref[...], preferred_element_type=F32)
        up = jnp.dot(xn, wu_ref[...], preferred_element_type=F32)
        hs.append((gate / (1.0 + jnp.exp(-gate)) * up).astype(BF16))
    return [x + 0.5 * jnp.dot(h, wd_ref[...], preferred_element_type=F32) for x, h in zip(xs, hs)]


def _sub_tiles(tm):
    rows = tm // TOKEN_SUB_TILES
    return [slice(s * rows, (s + 1) * rows) for s in range(TOKEN_SUB_TILES)]


def _group_mean_sq(z, gmat):
    sq = z * z
    hi = sq.astype(BF16)
    lo = (sq - hi.astype(F32)).astype(BF16)
    return (jnp.dot(hi, gmat, preferred_element_type=F32)
            + jnp.dot(lo, gmat, preferred_element_type=F32))


def _head_norm(z, gain, gmat):
    return z * lax.rsqrt(_group_mean_sq(z, gmat) + EPS) * gain


def _rope(x, cos, sin, first_half):
    rot = jnp.where(first_half, pltpu.roll(x, LANES - HEAD_DIM // 2, 1), pltpu.roll(x, HEAD_DIM // 2, 1))
    return x * cos + rot * sin


def _project_rows(z, rows, gmat_ref, cos_ref, sin_ref, naq_ref, nak_ref, sgv_ref, gq_ref, gk_ref, sgw_ref, sgb_ref,
                  qa_ref, ka_ref, va_ref, ob_ref, qc_ref, kc_ref, vct_ref):
    r = z.shape[0]
    gmat = gmat_ref[...]
    gmat_half = gmat_ref[0:LANES, 0:LANES]

    c0 = 0
    qa = _head_norm(z[:, c0:c0 + NA_W], naq_ref[...], gmat)
    qa_ref[rows, :] = (qa * (HEAD_DIM ** -0.5 * LOG2E)).astype(BF16)
    c0 += NA_W
    ka_ref[rows, :] = _head_norm(z[:, c0:c0 + NA_W], nak_ref[...], gmat).astype(BF16)
    c0 += NA_W
    va_ref[rows, :] = z[:, c0:c0 + NA_W].astype(BF16)
    c0 += NA_W
    u = _gelu(z[:, c0:c0 + SG_W])
    c0 += SG_W
    vn = _head_norm(_gelu(z[:, c0:c0 + SG_W]), sgv_ref[...], gmat)
    c0 += SG_W

    lane = lax.broadcasted_iota(jnp.int32, (r, LANES), 1)
    first_half = (lane % HEAD_DIM) < (HEAD_DIM // 2)
    cos = cos_ref[rows, :]
    sin = sin_ref[rows, :]
    gq = jnp.concatenate([gq_ref[...], gq_ref[...]], axis=-1)
    for s2 in range(GQ_W // (2 * LANES)):
        qn2 = _head_norm(z[:, c0 + s2 * 2 * LANES:c0 + (s2 + 1) * 2 * LANES], gq, gmat)
        for s in range(2):
            qr = _rope(qn2[:, s * LANES:(s + 1) * LANES], cos, sin, first_half) * (HEAD_DIM ** -0.5 * LOG2E)
            qc_ref[rows, (2 * s2 + s) * LANES:(2 * s2 + s + 1) * LANES] = qr.astype(BF16)
    c0 += GQ_W
    kn = _head_norm(z[:, c0:c0 + GKV_W], gk_ref[...], gmat_half)
    kc_ref[rows, :] = _rope(kn, cos, sin, first_half).astype(BF16)
    c0 += GKV_W
    vct_ref[:, rows] = z[:, c0:c0 + GKV_W].T.astype(BF16)

    lane_c = lax.broadcasted_iota(jnp.int32, (SG_CHUNK, LANES), 1)
    lo = lane_c < HEAD_DIM
    for ch in range(r // SG_CHUNK):
        cr = slice(ch * SG_CHUNK, (ch + 1) * SG_CHUNK)
        mixed = []
        for pair in range(SG_GROUPS // 2):
            v128 = vn[cr, pair * LANES:(pair + 1) * LANES]
            rhs = jnp.concatenate([jnp.where(lo, v128, 0.0), jnp.where(lo, 0.0, v128)], axis=0)
            mixed.append(jnp.dot(sgw_ref[pair], rhs.astype(BF16), preferred_element_type=F32))
        mixed = jnp.concatenate(mixed, axis=-1) + sgb_ref[...]
        ob_ref[rows.start + ch * SG_CHUNK:rows.start + (ch + 1) * SG_CHUNK, :] = (u[cr, :] * mixed).astype(BF16)


def _ffn_proj_kernel(*refs, first_tiles):
    if first_tiles is None:
        x_ref, refs = refs[0], refs[1:]
        load = lambda rows: x_ref[rows, :]
    else:
        x_ref, x2_ref, refs = refs[0], refs[1], refs[2:]
        from_first = pl.program_id(0) < first_tiles
        load = lambda rows: jnp.where(from_first, x_ref[rows, :], x2_ref[rows, :])
    g1_ref, wg_ref, wu_ref, wd_ref, gm_ref, win_ref = refs[:6]
    tables, xo_ref, outs = refs[6:16], refs[16], refs[17:]
    subs = _sub_tiles(xo_ref.shape[0])
    xs = _ffn([load(rows) for rows in subs], g1_ref, wg_ref, wu_ref, wd_ref)
    zs = []
    for rows, x in zip(subs, xs):
        xo_ref[rows, :] = x
        h = _rms(x, gm_ref[...]).astype(BF16)
        zs.append(jnp.dot(h, win_ref[...], preferred_element_type=F32))
    for rows, z in zip(subs, zs):
        _project_rows(z, rows, *tables, *outs)


def _ffn_proj(x, l, p, seq, x2=None):
    tm = TOKEN_TILE
    n = x.shape[0] + (0 if x2 is None else x2.shape[0])
    nt = seq // tm
    row = lambda i: (i, 0)
    const2 = lambda i: (0, 0)
    lsel3 = lambda i: (l, 0, 0)
    if x2 is None:
        first_tiles = None
        x_args = (x,)
        x_specs = [pl.BlockSpec((tm, D_MODEL), row)]
    else:
        first_tiles = x.shape[0] // tm
        x_args = (x, x2)
        x_specs = [pl.BlockSpec((tm, D_MODEL), lambda i: (jnp.minimum(i, first_tiles - 1), 0)),
                   pl.BlockSpec((tm, D_MODEL), lambda i: (jnp.maximum(i - first_tiles, 0), 0))]
    in_specs = x_specs + [
        _const_spec((None, 1, D_MODEL), lsel3),
        _const_spec((None, D_MODEL, D_FF), lsel3),
        _const_spec((None, D_MODEL, D_FF), lsel3),
        _const_spec((None, D_FF, D_MODEL), lsel3),
        _const_spec((None, 1, D_MODEL), lsel3),
        _const_spec((None, D_MODEL, D_IN), lsel3),
        _const_spec((2 * LANES, 2 * LANES), const2),
        pl.BlockSpec((tm, LANES), lambda i: (i % nt, 0)),
        pl.BlockSpec((tm, LANES), lambda i: (i % nt, 0)),
        _const_spec((None, 1, NA_W), lsel3),
        _const_spec((None, 1, NA_W), lsel3),
        _const_spec((None, 1, SG_W), lsel3),
        _const_spec((None, 1, LANES), lsel3),
        _const_spec((None, 1, LANES), lsel3),
        _const_spec((None, SG_GROUPS // 2, SG_CHUNK, 2 * SG_CHUNK), lambda i: (l, 0, 0, 0)),
        _const_spec((None, SG_CHUNK, SG_W), lsel3),
    ]
    out_shape = [
        jax.ShapeDtypeStruct((n, D_MODEL), F32),
        jax.ShapeDtypeStruct((n, NA_W), BF16),
        jax.ShapeDtypeStruct((n, NA_W), BF16),
        jax.ShapeDtypeStruct((n, NA_W), BF16),
        jax.ShapeDtypeStruct((n, SG_W), BF16),
        jax.ShapeDtypeStruct((n, GQ_W), BF16),
        jax.ShapeDtypeStruct((n, GKV_W), BF16),
        jax.ShapeDtypeStruct((GKV_W, n), BF16),
    ]
    out_specs = [
        pl.BlockSpec((tm, D_MODEL), row),
        pl.BlockSpec((tm, NA_W), row),
        pl.BlockSpec((tm, NA_W), row),
        pl.BlockSpec((tm, NA_W), row),
        pl.BlockSpec((tm, SG_W), row),
        pl.BlockSpec((tm, GQ_W), row),
        pl.BlockSpec((tm, GKV_W), row),
        pl.BlockSpec((GKV_W, tm), lambda i: (0, i)),
    ]
    return pl.pallas_call(
        functools.partial(_ffn_proj_kernel, first_tiles=first_tiles),
        grid=(n // tm,),
        in_specs=in_specs,
        out_specs=out_specs,
        out_shape=out_shape,
        compiler_params=pltpu.CompilerParams(
            dimension_semantics=("parallel",), vmem_limit_bytes=VMEM_LIMIT_BYTES),
        name="ffn_proj",
    )(*x_args, p["ffn1_norm"], p["ffn1_w_gate"], p["ffn1_w_up"], p["ffn1_w_down"], p["mix_norm"], p["w_in"],
      p["gmat"], p["cos"], p["sin"], p["na_q_gain"], p["na_k_gain"], p["sg_v_gain"], p["gqa_q_gain"],
      p["gqa_k_gain"], p["sg_w_pairs"], p["sg_bias"])


def _na_kernel(q_ref, k_ref, v_ref, bias_ref, o_ref, *, rows_per_step, grid_rows):
    i = pl.program_id(1)
    lane_head = lax.broadcasted_iota(jnp.int32, (GRID_W, NA_W), 1) // HEAD_DIM
    head_mask = [lane_head == h for h in range(NA_HEADS)]

    rows = []
    for rr in range(rows_per_step):
        r = i * rows_per_step + rr
        rs = jnp.clip(r - WIN_R // 2, 0, grid_rows - WIN_R)
        rows.append((rr * GRID_W, pl.multiple_of(rs * GRID_W, GRID_W), r - rs))

    scores = []
    for q0, k0, t in rows:
        q_row = q_ref[q0:q0 + GRID_W, :]
        zero = jnp.zeros_like(q_row)
        q_blk = jnp.concatenate([jnp.where(head_mask[h], q_row, zero) for h in range(NA_HEADS)], axis=0)
        kw = k_ref[pl.ds(k0, NA_KEYS), :]
        scores.append(lax.dot_general(q_blk, kw, (((1,), (1,)), ((), ())), preferred_element_type=F32))

    probs = []
    for (q0, k0, t), s in zip(rows, scores):
        s = s + bias_ref[t]
        m = jnp.max(s, axis=-1, keepdims=True)
        e = jnp.exp2(s - m)
        probs.append((e.astype(BF16), jnp.sum(e, axis=-1, keepdims=True)))

    for (q0, k0, t), (e, denom) in zip(rows, probs):
        vw = v_ref[pl.ds(k0, NA_KEYS), :]
        pv = jnp.dot(e, vw, preferred_element_type=F32) / denom
        out = jnp.where(head_mask[0], pv[0:GRID_W], 0.0)
        for h in range(1, NA_HEADS):
            out = jnp.where(head_mask[h], pv[h * GRID_W:(h + 1) * GRID_W], out)
        o_ref[q0:q0 + GRID_W, :] = out.astype(BF16)


def _na(qa, ka, va, bias_l, batch, seq):
    n = qa.shape[0]
    grid_rows = seq // GRID_W
    rps = NA_ROWS_PER_STEP
    steps = grid_rows // rps
    blk = rps * GRID_W
    return pl.pallas_call(
        functools.partial(_na_kernel, rows_per_step=rps, grid_rows=grid_rows),
        grid=(batch, steps),
        in_specs=[
            pl.BlockSpec((blk, NA_W), lambda b, i: (b * steps + i, 0)),
            pl.BlockSpec((seq, NA_W), lambda b, i: (b, 0)),
            pl.BlockSpec((seq, NA_W), lambda b, i: (b, 0)),
            _const_spec((WIN_R, NA_HEADS * GRID_W, NA_KEYS), lambda b, i: (0, 0, 0)),
        ],
        out_specs=pl.BlockSpec((blk, NA_W), lambda b, i: (b * steps + i, 0)),
        out_shape=jax.ShapeDtypeStruct((n, NA_W), BF16),
        compiler_params=pltpu.CompilerParams(
            dimension_semantics=("parallel", "parallel"), vmem_limit_bytes=VMEM_LIMIT_BYTES),
        name="na",
    )(qa, ka, va, bias_l)


def _na_bias_table(rpb):
    c = np.arange(GRID_W)
    kc = np.arange(GRID_W)
    cs = np.clip(c - WIN_C // 2, 0, GRID_W - WIN_C)
    valid = (kc[None, :] >= cs[:, None]) & (kc[None, :] < cs[:, None] + WIN_C)
    dc = kc[None, :] - c[:, None] + (WIN_C - 1)
    t = np.arange(WIN_R)
    i = np.arange(WIN_R)
    dr = i[None, :] - t[:, None] + (WIN_R - 1)
    pick_col = ((dc[None] == np.arange(2 * WIN_C - 1)[:, None, None]) & valid[None]).astype(np.float32)
    pick_row = (dr[None] == np.arange(2 * WIN_R - 1)[:, None, None]).astype(np.float32)
    tbl = jnp.einsum("hrd,rti,dck->thcik", rpb.astype(F32), pick_row, pick_col,
                     precision=lax.Precision.HIGHEST) * LOG2E
    tbl = jnp.where(valid[None, None, :, None, :], tbl, NEG_BIG)
    return tbl.reshape(WIN_R, NA_HEADS * GRID_W, NA_KEYS)


def _gqa_kernel(q_ref, k_ref, vt_ref, o_ref, qt_scr, s_scr, mc_scr, m_scr, acc_scr, *, tq, tk, seq):
    g = pl.program_id(1)
    m_lanes = GQA_REP * tq
    n_kv = seq // tk

    qt = q_ref[...].astype(F32).T
    qt = jnp.concatenate([qt[h * HEAD_DIM:(h + 1) * HEAD_DIM, :] for h in range(GQA_REP)], axis=1)
    zero = jnp.zeros_like(qt)
    qt_scr[...] = jnp.concatenate([jnp.where(g == 0, qt, zero), jnp.where(g == 1, qt, zero)],
                                  axis=0).astype(BF16)
    m_scr[...] = jnp.full(m_scr.shape, NEG_BIG, F32)
    acc_scr[...] = jnp.zeros(acc_scr.shape, F32)
    ones = jnp.ones((ONES_ROWS, tk), BF16)
    v_row0 = pl.multiple_of(g * HEAD_DIM, HEAD_DIM)

    chunks = [slice(c * GQA_CHUNK, (c + 1) * GQA_CHUNK) for c in range(m_lanes // GQA_CHUNK)]

    def scores(j, slot):
        k0 = pl.multiple_of(j * tk, tk)
        kj = k_ref[pl.ds(k0, tk), :]
        for c, cols in enumerate(chunks):
            st = jnp.dot(kj, qt_scr[:, cols], preferred_element_type=F32)
            s_scr[slot, c] = st
            mc_scr[slot, :, cols] = jnp.max(st, axis=0, keepdims=True)

    def softmax_pv(j, slot):
        k0 = pl.multiple_of(j * tk, tk)
        vext = jnp.concatenate([vt_ref[pl.ds(v_row0, HEAD_DIM), pl.ds(k0, tk)], ones], axis=0)
        for c, cols in enumerate(chunks):
            m_old = m_scr[:, cols]
            m_new = jnp.maximum(m_old, mc_scr[slot, :, cols])
            alpha = jnp.exp2(m_old - m_new)
            pt = jnp.exp2(s_scr[slot, c] - m_new).astype(BF16)
            acc_scr[:, cols] = acc_scr[:, cols] * alpha + jnp.dot(vext, pt, preferred_element_type=F32)
            m_scr[:, cols] = m_new

    scores(0, 0)

    def kv_steps(jj, carry):
        for u in range(GQA_UNROLL // 2):
            j0 = GQA_UNROLL * jj + 2 * u
            scores(j0 + 1, 1)
            softmax_pv(j0, 0)
            scores(jnp.minimum(j0 + 2, n_kv - 1), 0)
            softmax_pv(j0 + 1, 1)
        return carry

    lax.fori_loop(0, n_kv // GQA_UNROLL, kv_steps, 0)

    acc = acc_scr[...]
    ot = acc[0:HEAD_DIM, :] / acc[HEAD_DIM:HEAD_DIM + 1, :]
    o = jnp.concatenate([ot[:, h * tq:(h + 1) * tq].T for h in range(GQA_REP)], axis=1)
    o_ref[...] = o.astype(BF16)


def _gqa(qc, kc, vct, batch, seq):
    n = qc.shape[0]
    tq, tk = GQA_TQ, GQA_TK
    assert seq % tq == 0 and seq % (tk * GQA_UNROLL) == 0
    nq = seq // tq
    qw = GQA_REP * HEAD_DIM
    return pl.pallas_call(
        functools.partial(_gqa_kernel, tq=tq, tk=tk, seq=seq),
        grid=(batch, GQA_KV_HEADS, nq),
        in_specs=[
            pl.BlockSpec((tq, qw), lambda b, g, i: (b * nq + i, g)),
            pl.BlockSpec((seq, GKV_W), lambda b, g, i: (b, 0)),
            pl.BlockSpec((GKV_W, seq), lambda b, g, i: (0, b)),
        ],
        out_specs=pl.BlockSpec((tq, qw), lambda b, g, i: (b * nq + i, g)),
        out_shape=jax.ShapeDtypeStruct((n, GQ_W), BF16),
        scratch_shapes=[
            pltpu.VMEM((GKV_W, GQA_REP * tq), BF16),
            pltpu.VMEM((2, GQA_REP * tq // GQA_CHUNK, tk, GQA_CHUNK), F32),
            pltpu.VMEM((2, 1, GQA_REP * tq), F32),
            pltpu.VMEM((1, GQA_REP * tq), F32),
            pltpu.VMEM((HEAD_DIM + ONES_ROWS, GQA_REP * tq), F32),
        ],
        compiler_params=pltpu.CompilerParams(
            dimension_semantics=("parallel", "parallel", "parallel"), vmem_limit_bytes=VMEM_LIMIT_BYTES),
        name="gqa",
    )(qc, kc, vct)


def _out_ffn_kernel(x_ref, a_ref, b_ref, c_ref, wo_ref, g2_ref, wg_ref, wu_ref, wd_ref, o_ref):
    subs = _sub_tiles(o_ref.shape[0])
    xs = []
    for rows in subs:
        mix = jnp.concatenate([a_ref[rows, :], b_ref[rows, :], c_ref[rows, :]], axis=-1)
        xs.append(x_ref[rows, :] + jnp.dot(mix, wo_ref[...], preferred_element_type=F32))
    for rows, y in zip(subs, _ffn(xs, g2_ref, wg_ref, wu_ref, wd_ref)):
        o_ref[rows, :] = y


def _out_ffn(x, oa, ob, oc, l, p, row0=0, n=None):
    n = x.shape[0] if n is None else n
    tm = TOKEN_TILE
    assert row0 % tm == 0 and n % tm == 0
    row = lambda i: (i + row0 // tm, 0)
    lsel3 = lambda i: (l, 0, 0)
    return pl.pallas_call(
        _out_ffn_kernel,
        grid=(n // tm,),
        in_specs=[
            pl.BlockSpec((tm, D_MODEL), row),
            pl.BlockSpec((tm, NA_W), row),
            pl.BlockSpec((tm, SG_W), row),
            pl.BlockSpec((tm, GQ_W), row),
            _const_spec((None, D_MODEL, D_MODEL), lsel3),
            _const_spec((None, 1, D_MODEL), lsel3),
            _const_spec((None, D_MODEL, D_FF), lsel3),
            _const_spec((None, D_MODEL, D_FF), lsel3),
            _const_spec((None, D_FF, D_MODEL), lsel3),
        ],
        out_specs=pl.BlockSpec((tm, D_MODEL), lambda i: (i, 0)),
        out_shape=jax.ShapeDtypeStruct((n, D_MODEL), F32),
        compiler_params=pltpu.CompilerParams(
            dimension_semantics=("parallel",), vmem_limit_bytes=VMEM_LIMIT_BYTES),
        name="out_ffn",
    )(x, oa, ob, oc, p["w_out"], p["ffn2_norm"], p["ffn2_w_gate"], p["ffn2_w_up"], p["ffn2_w_down"])


def _rope_tables(seq):
    t = np.arange(seq)
    row = (t // GRID_W).astype(np.float32)
    col = (t % GRID_W).astype(np.float32)
    n_freq = HEAD_DIM // 4
    inv = jnp.asarray(ROPE_THETA, F32) ** (-jnp.arange(n_freq, dtype=F32) / n_freq)
    ang = jnp.concatenate([jnp.asarray(row)[:, None] * inv, jnp.asarray(col)[:, None] * inv], axis=-1)
    cos, sin = jnp.cos(ang), jnp.sin(ang)
    reps = LANES // HEAD_DIM
    cos_t = jnp.tile(jnp.concatenate([cos, cos], axis=-1), (1, reps))
    sin_t = jnp.tile(jnp.concatenate([-sin, sin], axis=-1), (1, reps))
    return cos_t, sin_t


def _prepare(ffn1_norm, ffn1_w_gate, ffn1_w_up, ffn1_w_down, mix_norm, w_in, na_q_norm, na_k_norm,
             na_rpb, sg_v_norm, sg_w, sg_b, gqa_q_norm, gqa_k_norm, w_out, ffn2_norm, ffn2_w_gate,
             ffn2_w_up, ffn2_w_down, seq):
    depth = w_in.shape[0]
    cos_t, sin_t = _rope_tables(seq)
    gm = np.kron(np.eye(2 * LANES // HEAD_DIM), np.ones((HEAD_DIM, HEAD_DIM))) / HEAD_DIM
    sg_pairs = jnp.concatenate([sg_w[:, 0::2], sg_w[:, 1::2]], axis=-1).astype(BF16)
    sg_bias = jnp.repeat(jnp.transpose(sg_b, (0, 2, 1)), HEAD_DIM, axis=-1)
    vec = lambda a, reps=1: jnp.tile(a.reshape(depth, 1, -1), (1, 1, reps))
    return dict(
        ffn1_norm=vec(ffn1_norm), ffn1_w_gate=ffn1_w_gate.astype(BF16), ffn1_w_up=ffn1_w_up.astype(BF16),
        ffn1_w_down=ffn1_w_down.astype(BF16), mix_norm=vec(mix_norm), w_in=w_in.astype(BF16),
        gmat=jnp.asarray(gm, BF16), cos=cos_t, sin=sin_t,
        na_q_gain=vec(na_q_norm, NA_HEADS), na_k_gain=vec(na_k_norm, NA_HEADS),
        sg_v_gain=vec(sg_v_norm),
        gqa_q_gain=vec(gqa_q_norm, LANES // HEAD_DIM), gqa_k_gain=vec(gqa_k_norm, LANES // HEAD_DIM),
        sg_w_pairs=sg_pairs, sg_bias=sg_bias,
        na_bias=jax.vmap(_na_bias_table)(na_rpb),
        w_out=w_out.astype(BF16), ffn2_norm=vec(ffn2_norm), ffn2_w_gate=ffn2_w_gate.astype(BF16),
        ffn2_w_up=ffn2_w_up.astype(BF16), ffn2_w_down=ffn2_w_down.astype(BF16),
    )


def _trunk(x_a, x_b, p, depth):
    seq = x_a.shape[1]
    assert x_b.shape[1] == seq and seq % TOKEN_TILE == 0 and seq % (GRID_W * NA_ROWS_PER_STEP) == 0
    n_a, n_b = x_a.shape[0] * seq, x_b.shape[0] * seq
    batch = x_a.shape[0] + x_b.shape[0]
    x = None
    for l in range(depth):
        if l == 0:
            outs = _ffn_proj(x_a.reshape(n_a, D_MODEL), l, p, seq, x2=x_b.reshape(n_b, D_MODEL))
        else:
            outs = _ffn_proj(x, l, p, seq)
        x, qa, ka, va, ob, qc, kc, vct = outs
        oa = _na(qa, ka, va, p["na_bias"][l], batch, seq)
        oc = _gqa(qc, kc, vct, batch, seq)
        if l + 1 < depth:
            x = _out_ffn(x, oa, ob, oc, l, p)
    y_a = _out_ffn(x, oa, ob, oc, depth - 1, p, row0=0, n=n_a)
    y_b = _out_ffn(x, oa, ob, oc, depth - 1, p, row0=n_a, n=n_b)
    return y_a.reshape(x_a.shape), y_b.reshape(x_b.shape)


def kernel(x_prompt, x_sample, ffn1_norm, ffn1_w_gate, ffn1_w_up, ffn1_w_down, mix_norm, w_in, na_q_norm, na_k_norm, na_rpb, sg_v_norm, sg_w, sg_b, gqa_q_norm, gqa_k_norm, w_out, ffn2_norm, ffn2_w_gate, ffn2_w_up, ffn2_w_down):
    p = _prepare(ffn1_norm, ffn1_w_gate, ffn1_w_up, ffn1_w_down, mix_norm, w_in, na_q_norm, na_k_norm,
                 na_rpb, sg_v_norm, sg_w, sg_b, gqa_q_norm, gqa_k_norm, w_out, ffn2_norm, ffn2_w_gate,
                 ffn2_w_up, ffn2_w_down, x_prompt.shape[1])
    return _trunk(x_prompt, x_sample, p, w_in.shape[0])
```

```python
import functools

import numpy as np
import jax
import jax.numpy as jnp
from jax import lax
from jax.experimental import pallas as pl
from jax.experimental.pallas import tpu as pltpu

F32 = jnp.float32
BF16 = jnp.bfloat16

D_MODEL = 1024
HEAD_DIM = 64
GRID_W = 64
NA_HEADS = 4
WIN_R = 8
WIN_C = 16
SG_GROUPS = 4
SG_CHUNK = 128
GQA_Q_HEADS = 8
GQA_KV_HEADS = 2
GQA_REP = GQA_Q_HEADS // GQA_KV_HEADS
ROPE_THETA = 10000.0
D_FF = 2816
EPS = 1e-6
LOG2E = 1.4426950408889634
NEG_BIG = -1e30

NA_W = NA_HEADS * HEAD_DIM
SG_W = SG_GROUPS * HEAD_DIM
GQ_W = GQA_Q_HEADS * HEAD_DIM
GKV_W = GQA_KV_HEADS * HEAD_DIM
D_IN = 3 * NA_W + 2 * SG_W + GQ_W + 2 * GKV_W
NA_KEYS = WIN_R * GRID_W

LANES = 128
VMEM_LIMIT_BYTES = 56 * 1024 * 1024

TOKEN_TILE = 512
TOKEN_SUB_TILES = 2
NA_ROWS_PER_STEP = 8
GQA_TQ = 512
GQA_TK = 256
GQA_CHUNK = 256
GQA_UNROLL = 16
GQA_SKEW = 4
GQA_MAX_SHIFT_GAP = 64.0
BF16_TILE_ROWS = 16
ONES_ROWS = BF16_TILE_ROWS


def _const_spec(shape, index_map):
    return pl.BlockSpec(shape, index_map, pipeline_mode=pl.Buffered(1))


def _rms(x, g):
    ms = jnp.mean(x * x, axis=-1, keepdims=True)
    return x * lax.rsqrt(ms + EPS) * g


def _gelu(x):
    return 0.5 * x * (1.0 + jnp.tanh(0.7978845608028654 * (x + 0.044715 * (x * x * x))))


def _ffn(xs, g_ref, wg_ref, wu_ref, wd_ref):
    hs = []
    for x in xs:
        xn = _rms(x, g_ref[...]).astype(BF16)
        gate = jnp.dot(xn, wg_ref[...], preferred_element_type=F32)
        up = jnp.dot(xn, wu_ref[...], preferred_element_type=F32)
        hs.append((gate / (1.0 + jnp.exp(-gate)) * up).astype(BF16))
    return [x + 0.5 * jnp.dot(h, wd_ref[...], preferred_element_type=F32) for x, h in zip(xs, hs)]


def _sub_tiles(tm):
    rows = tm // TOKEN_SUB_TILES
    return [slice(s * rows, (s + 1) * rows) for s in range(TOKEN_SUB_TILES)]


def _group_mean_sq(z, gmat):
    sq = z * z
    hi = sq.astype(BF16)
    lo = (sq - hi.astype(F32)).astype(BF16)
    return (jnp.dot(hi, gmat, preferred_element_type=F32)
            + jnp.dot(lo, gmat, preferred_element_type=F32))


def _head_norm(z, gain, gmat):
    return z * lax.rsqrt(_group_mean_sq(z, gmat) + EPS) * gain


def _rope(x, cos, sin, first_half):
    rot = jnp.where(first_half, pltpu.roll(x, LANES - HEAD_DIM // 2, 1), pltpu.roll(x, HEAD_DIM // 2, 1))
    return x * cos + rot * sin


def _project_rows(z, rows, gmat_ref, cos_ref, sin_ref, naq_ref, nak_ref, sgv_ref, gq_ref, gk_ref, sgw_ref, sgb_ref,
                  qa_ref, ka_ref, va_ref, ob_ref, qc_ref, kc_ref, vct_ref):
    r = z.shape[0]
    gmat = gmat_ref[...]
    gmat_half = gmat_ref[0:LANES, 0:LANES]

    c0 = 0
    qa = _head_norm(z[:, c0:c0 + NA_W], naq_ref[...], gmat)
    qa_ref[rows, :] = (qa * (HEAD_DIM ** -0.5 * LOG2E)).astype(BF16)
    c0 += NA_W
    ka_ref[rows, :] = _head_norm(z[:, c0:c0 + NA_W], nak_ref[...], gmat).astype(BF16)
    c0 += NA_W
    va_ref[rows, :] = z[:, c0:c0 + NA_W].astype(BF16)
    c0 += NA_W
    u = _gelu(z[:, c0:c0 + SG_W])
    c0 += SG_W
    vn = _head_norm(_gelu(z[:, c0:c0 + SG_W]), sgv_ref[...], gmat)
    c0 += SG_W

    lane = lax.broadcasted_iota(jnp.int32, (r, LANES), 1)
    first_half = (lane % HEAD_DIM) < (HEAD_DIM // 2)
    cos = cos_ref[rows, :]
    sin = sin_ref[rows, :]
    gq = jnp.concatenate([gq_ref[...], gq_ref[...]], axis=-1)
    for s2 in range(GQ_W // (2 * LANES)):
        qn2 = _head_norm(z[:, c0 + s2 * 2 * LANES:c0 + (s2 + 1) * 2 * LANES], gq, gmat)
        for s in range(2):
            qr = _rope(qn2[:, s * LANES:(s + 1) * LANES], cos, sin, first_half) * (HEAD_DIM ** -0.5 * LOG2E)
            qc_ref[rows, (2 * s2 + s) * LANES:(2 * s2 + s + 1) * LANES] = qr.astype(BF16)
    c0 += GQ_W
    kn = _head_norm(z[:, c0:c0 + GKV_W], gk_ref[...], gmat_half)
    kc_ref[rows, :] = _rope(kn, cos, sin, first_half).astype(BF16)
    c0 += GKV_W
    vct_ref[:, rows] = z[:, c0:c0 + GKV_W].T.astype(BF16)

    lane_c = lax.broadcasted_iota(jnp.int32, (SG_CHUNK, LANES), 1)
    lo = lane_c < HEAD_DIM
    for ch in range(r // SG_CHUNK):
        cr = slice(ch * SG_CHUNK, (ch + 1) * SG_CHUNK)
        mixed = []
        for pair in range(SG_GROUPS // 2):
            v128 = vn[cr, pair * LANES:(pair + 1) * LANES]
            rhs = jnp.concatenate([jnp.where(lo, v128, 0.0), jnp.where(lo, 0.0, v128)], axis=0)
            mixed.append(jnp.dot(sgw_ref[pair], rhs.astype(BF16), preferred_element_type=F32))
        mixed = jnp.concatenate(mixed, axis=-1) + sgb_ref[...]
        ob_ref[rows.start + ch * SG_CHUNK:rows.start + (ch + 1) * SG_CHUNK, :] = (u[cr, :] * mixed).astype(BF16)


def _ffn_proj_kernel(*refs, first_tiles):
    if first_tiles is None:
        x_ref, refs = refs[0], refs[1:]
        load = lambda rows: x_ref[rows, :]
    else:
        x_ref, x2_ref, refs = refs[0], refs[1], refs[2:]
        from_first = pl.program_id(0) < first_tiles
        load = lambda rows: jnp.where(from_first, x_ref[rows, :], x2_ref[rows, :])
    g1_ref, wg_ref, wu_ref, wd_ref, gm_ref, win_ref = refs[:6]
    tables, xo_ref, outs = refs[6:16], refs[16], refs[17:]
    subs = _sub_tiles(xo_ref.shape[0])
    xs = _ffn([load(rows) for rows in subs], g1_ref, wg_ref, wu_ref, wd_ref)
    zs = []
    for rows, x in zip(subs, xs):
        xo_ref[rows, :] = x
        h = _rms(x, gm_ref[...]).astype(BF16)
        zs.append(jnp.dot(h, win_ref[...], preferred_element_type=F32))
    for rows, z in zip(subs, zs):
        _project_rows(z, rows, *tables, *outs)


def _ffn_proj(x, l, p, seq, x2=None):
    tm = TOKEN_TILE
    n = x.shape[0] + (0 if x2 is None else x2.shape[0])
    nt = seq // tm
    row = lambda i: (i, 0)
    const2 = lambda i: (0, 0)
    lsel3 = lambda i: (l, 0, 0)
    if x2 is None:
        first_tiles = None
        x_args = (x,)
        x_specs = [pl.BlockSpec((tm, D_MODEL), row)]
    else:
        first_tiles = x.shape[0] // tm
        x_args = (x, x2)
        x_specs = [pl.BlockSpec((tm, D_MODEL), lambda i: (jnp.minimum(i, first_tiles - 1), 0)),
                   pl.BlockSpec((tm, D_MODEL), lambda i: (jnp.maximum(i - first_tiles, 0), 0))]
    in_specs = x_specs + [
        _const_spec((None, 1, D_MODEL), lsel3),
        _const_spec((None, D_MODEL, D_FF), lsel3),
        _const_spec((None, D_MODEL, D_FF), lsel3),
        _const_spec((None, D_FF, D_MODEL), lsel3),
        _const_spec((None, 1, D_MODEL), lsel3),
        _const_spec((None, D_MODEL, D_IN), lsel3),
        _const_spec((2 * LANES, 2 * LANES), const2),
        pl.BlockSpec((tm, LANES), lambda i: (i % nt, 0)),
        pl.BlockSpec((tm, LANES), lambda i: (i % nt, 0)),
        _const_spec((None, 1, NA_W), lsel3),
        _const_spec((None, 1, NA_W), lsel3),
        _const_spec((None, 1, SG_W), lsel3),
        _const_spec((None, 1, LANES), lsel3),
        _const_spec((None, 1, LANES), lsel3),
        _const_spec((None, SG_GROUPS // 2, SG_CHUNK, 2 * SG_CHUNK), lambda i: (l, 0, 0, 0)),
        _const_spec((None, SG_CHUNK, SG_W), lsel3),
    ]
    out_shape = [
        jax.ShapeDtypeStruct((n, D_MODEL), F32),
        jax.ShapeDtypeStruct((n, NA_W), BF16),
        jax.ShapeDtypeStruct((n, NA_W), BF16),
        jax.ShapeDtypeStruct((n, NA_W), BF16),
        jax.ShapeDtypeStruct((n, SG_W), BF16),
        jax.ShapeDtypeStruct((n, GQ_W), BF16),
        jax.ShapeDtypeStruct((n, GKV_W), BF16),
        jax.ShapeDtypeStruct((GKV_W, n), BF16),
    ]
    out_specs = [
        pl.BlockSpec((tm, D_MODEL), row),
        pl.BlockSpec((tm, NA_W), row),
        pl.BlockSpec((tm, NA_W), row),
        pl.BlockSpec((tm, NA_W), row),
        pl.BlockSpec((tm, SG_W), row),
        pl.BlockSpec((tm, GQ_W), row),
        pl.BlockSpec((tm, GKV_W), row),
        pl.BlockSpec((GKV_W, tm), lambda i: (0, i)),
    ]
    return pl.pallas_call(
        functools.partial(_ffn_proj_kernel, first_tiles=first_tiles),
        grid=(n // tm,),
        in_specs=in_specs,
        out_specs=out_specs,
        out_shape=out_shape,
        compiler_params=pltpu.CompilerParams(
            dimension_semantics=("parallel",), vmem_limit_bytes=VMEM_LIMIT_BYTES),
        name="ffn_proj",
    )(*x_args, p["ffn1_norm"], p["ffn1_w_gate"], p["ffn1_w_up"], p["ffn1_w_down"], p["mix_norm"], p["w_in"],
      p["gmat"], p["cos"], p["sin"], p["na_q_gain"], p["na_k_gain"], p["sg_v_gain"], p["gqa_q_gain"],
      p["gqa_k_gain"], p["sg_w_pairs"], p["sg_bias"])


def _na_kernel(q_ref, k_ref, v_ref, bias_ref, o_ref, *, rows_per_step, grid_rows):
    i = pl.program_id(1)
    lane_head = lax.broadcasted_iota(jnp.int32, (GRID_W, NA_W), 1) // HEAD_DIM
    head_mask = [lane_head == h for h in range(NA_HEADS)]

    rows = []
    for rr in range(rows_per_step):
        r = i * rows_per_step + rr
        rs = jnp.clip(r - WIN_R // 2, 0, grid_rows - WIN_R)
        rows.append((rr * GRID_W, pl.multiple_of(rs * GRID_W, GRID_W), r - rs))

    scores = []
    for q0, k0, t in rows:
        q_row = q_ref[q0:q0 + GRID_W, :]
        zero = jnp.zeros_like(q_row)
        q_blk = jnp.concatenate([jnp.where(head_mask[h], q_row, zero) for h in range(NA_HEADS)], axis=0)
        kw = k_ref[pl.ds(k0, NA_KEYS), :]
        scores.append(lax.dot_general(q_blk, kw, (((1,), (1,)), ((), ())), preferred_element_type=F32))

    probs = []
    for (q0, k0, t), s in zip(rows, scores):
        s = s + bias_ref[t]
        m = jnp.max(s, axis=-1, keepdims=True)
        e = jnp.exp2(s - m)
        probs.append((e.astype(BF16), jnp.sum(e, axis=-1, keepdims=True)))

    for (q0, k0, t), (e, denom) in zip(rows, probs):
        vw = v_ref[pl.ds(k0, NA_KEYS), :]
        pv = jnp.dot(e, vw, preferred_element_type=F32) / denom
        out = jnp.where(head_mask[0], pv[0:GRID_W], 0.0)
        for h in range(1, NA_HEADS):
            out = jnp.where(head_mask[h], pv[h * GRID_W:(h + 1) * GRID_W], out)
        o_ref[q0:q0 + GRID_W, :] = out.astype(BF16)


def _na(qa, ka, va, bias_l, batch, seq):
    n = qa.shape[0]
    grid_rows = seq // GRID_W
    rps = NA_ROWS_PER_STEP
    steps = grid_rows // rps
    blk = rps * GRID_W
    return pl.pallas_call(
        functools.partial(_na_kernel, rows_per_step=rps, grid_rows=grid_rows),
        grid=(batch, steps),
        in_specs=[
            pl.BlockSpec((blk, NA_W), lambda b, i: (b * steps + i, 0)),
            pl.BlockSpec((seq, NA_W), lambda b, i: (b, 0)),
            pl.BlockSpec((seq, NA_W), lambda b, i: (b, 0)),
            _const_spec((WIN_R, NA_HEADS * GRID_W, NA_KEYS), lambda b, i: (0, 0, 0)),
        ],
        out_specs=pl.BlockSpec((blk, NA_W), lambda b, i: (b * steps + i, 0)),
        out_shape=jax.ShapeDtypeStruct((n, NA_W), BF16),
        compiler_params=pltpu.CompilerParams(
            dimension_semantics=("parallel", "parallel"), vmem_limit_bytes=VMEM_LIMIT_BYTES),
        name="na",
    )(qa, ka, va, bias_l)


def _na_bias_table(rpb):
    c = np.arange(GRID_W)
    kc = np.arange(GRID_W)
    cs = np.clip(c - WIN_C // 2, 0, GRID_W - WIN_C)
    valid = (kc[None, :] >= cs[:, None]) & (kc[None, :] < cs[:, None] + WIN_C)
    dc = kc[None, :] - c[:, None] + (WIN_C - 1)
    t = np.arange(WIN_R)
    i = np.arange(WIN_R)
    dr = i[None, :] - t[:, None] + (WIN_R - 1)
    pick_col = ((dc[None] == np.arange(2 * WIN_C - 1)[:, None, None]) & valid[None]).astype(np.float32)
    pick_row = (dr[None] == np.arange(2 * WIN_R - 1)[:, None, None]).astype(np.float32)
    tbl = jnp.einsum("hrd,rti,dck->thcik", rpb.astype(F32), pick_row, pick_col,
                     precision=lax.Precision.HIGHEST) * LOG2E
    tbl = jnp.where(valid[None, None, :, None, :], tbl, NEG_BIG)
    return tbl.reshape(WIN_R, NA_HEADS * GRID_W, NA_KEYS)


def _gqa_kernel(q_ref, k_ref, vt_ref, o_ref, qt_scr, s_scr, mc_scr, m_scr, acc_scr, gap_scr, *, tq, tk, seq):
    g = pl.program_id(1)
    m_lanes = GQA_REP * tq
    n_kv = seq // tk

    qt = q_ref[...].astype(F32).T
    qt = jnp.concatenate([qt[h * HEAD_DIM:(h + 1) * HEAD_DIM, :] for h in range(GQA_REP)], axis=1)
    zero = jnp.zeros_like(qt)
    qt_scr[...] = jnp.concatenate([jnp.where(g == 0, qt, zero), jnp.where(g == 1, qt, zero)],
                                  axis=0).astype(BF16)
    ones = jnp.ones((ONES_ROWS, tk), BF16)
    v_row0 = pl.multiple_of(g * HEAD_DIM, HEAD_DIM)
    chunks = [slice(c * GQA_CHUNK, (c + 1) * GQA_CHUNK) for c in range(m_lanes // GQA_CHUNK)]

    def load_k(j):
        return k_ref[pl.ds(pl.multiple_of(j * tk, tk), tk), :]

    def load_v_ones(j):
        vt = vt_ref[pl.ds(v_row0, HEAD_DIM), pl.ds(pl.multiple_of(j * tk, tk), tk)]
        return jnp.concatenate([vt, ones], axis=0)

    acc_scr[...] = jnp.zeros(acc_scr.shape, F32)
    gap_scr[...] = jnp.zeros(gap_scr.shape, F32)
    for cols in chunks:
        s_first = jnp.dot(k_ref[0:BF16_TILE_ROWS, :], qt_scr[:, cols], preferred_element_type=F32)
        m_scr[:, cols] = jnp.max(s_first, axis=0, keepdims=True)

    def single_pass_tiles(jj, carry):
        pending = []
        for u in range(GQA_UNROLL):
            kj, vext = load_k(jj * GQA_UNROLL + u), load_v_ones(jj * GQA_UNROLL + u)
            for cols in chunks:
                st = jnp.dot(kj, qt_scr[:, cols], preferred_element_type=F32)

                def consume(st=st, vext=vext, cols=cols):
                    shift = m_scr[:, cols]
                    tile_max = jnp.max(st, axis=0, keepdims=True)
                    pt = jnp.exp2(st - shift).astype(BF16)
                    m_new = jnp.maximum(shift, tile_max)
                    acc_scr[:, cols] = ((acc_scr[:, cols] + jnp.dot(vext, pt, preferred_element_type=F32))
                                        * jnp.exp2(shift - m_new))
                    m_scr[:, cols] = m_new
                    gap_scr[:, cols] = jnp.maximum(gap_scr[:, cols], tile_max - shift)

                pending.append(consume)
                if len(pending) > GQA_SKEW:
                    pending.pop(0)()
        for consume in pending:
            consume()
        return carry

    lax.fori_loop(0, n_kv // GQA_UNROLL, single_pass_tiles, 0)

    def scores(j, slot):
        kj = load_k(j)
        for c, cols in enumerate(chunks):
            st = jnp.dot(kj, qt_scr[:, cols], preferred_element_type=F32)
            s_scr[slot, c] = st
            mc_scr[slot, :, cols] = jnp.max(st, axis=0, keepdims=True)

    def softmax_pv(j, slot):
        vext = load_v_ones(j)
        for c, cols in enumerate(chunks):
            m_old = m_scr[:, cols]
            m_new = jnp.maximum(m_old, mc_scr[slot, :, cols])
            alpha = jnp.exp2(m_old - m_new)
            pt = jnp.exp2(s_scr[slot, c] - m_new).astype(BF16)
            acc_scr[:, cols] = acc_scr[:, cols] * alpha + jnp.dot(vext, pt, preferred_element_type=F32)
            m_scr[:, cols] = m_new

    @pl.when(jnp.max(gap_scr[...]) > GQA_MAX_SHIFT_GAP)
    def _recompute_two_pass():
        m_scr[...] = jnp.full(m_scr.shape, NEG_BIG, F32)
        acc_scr[...] = jnp.zeros(acc_scr.shape, F32)
        scores(0, 0)

        def two_pass_tiles(jj, carry):
            for u in range(GQA_UNROLL // 2):
                j0 = GQA_UNROLL * jj + 2 * u
                scores(j0 + 1, 1)
                softmax_pv(j0, 0)
                scores(jnp.minimum(j0 + 2, n_kv - 1), 0)
                softmax_pv(j0 + 1, 1)
            return carry

        lax.fori_loop(0, n_kv // GQA_UNROLL, two_pass_tiles, 0)

    acc = acc_scr[...]
    ot = acc[0:HEAD_DIM, :] / acc[HEAD_DIM:HEAD_DIM + 1, :]
    o = jnp.concatenate([ot[:, h * tq:(h + 1) * tq].T for h in range(GQA_REP)], axis=1)
    o_ref[...] = o.astype(BF16)


def _gqa(qc, kc, vct, batch, seq):
    n = qc.shape[0]
    tq, tk = GQA_TQ, GQA_TK
    assert seq % tq == 0 and seq % (tk * GQA_UNROLL) == 0
    nq = seq // tq
    qw = GQA_REP * HEAD_DIM
    return pl.pallas_call(
        functools.partial(_gqa_kernel, tq=tq, tk=tk, seq=seq),
        grid=(batch, GQA_KV_HEADS, nq),
        in_specs=[
            pl.BlockSpec((tq, qw), lambda b, g, i: (b * nq + i, g)),
            pl.BlockSpec((seq, GKV_W), lambda b, g, i: (b, 0)),
            pl.BlockSpec((GKV_W, seq), lambda b, g, i: (0, b)),
        ],
        out_specs=pl.BlockSpec((tq, qw), lambda b, g, i: (b * nq + i, g)),
        out_shape=jax.ShapeDtypeStruct((n, GQ_W), BF16),
        scratch_shapes=[
            pltpu.VMEM((GKV_W, GQA_REP * tq), BF16),
            pltpu.VMEM((2, GQA_REP * tq // GQA_CHUNK, tk, GQA_CHUNK), F32),
            pltpu.VMEM((2, 1, GQA_REP * tq), F32),
            pltpu.VMEM((1, GQA_REP * tq), F32),
            pltpu.VMEM((HEAD_DIM + ONES_ROWS, GQA_REP * tq), F32),
            pltpu.VMEM((1, GQA_REP * tq), F32),
        ],
        compiler_params=pltpu.CompilerParams(
            dimension_semantics=("parallel", "parallel", "parallel"), vmem_limit_bytes=VMEM_LIMIT_BYTES),
        name="gqa",
    )(qc, kc, vct)


def _out_ffn_kernel(x_ref, a_ref, b_ref, c_ref, wo_ref, g2_ref, wg_ref, wu_ref, wd_ref, o_ref):
    subs = _sub_tiles(o_ref.shape[0])
    xs = []
    for rows in subs:
        mix = jnp.concatenate([a_ref[rows, :], b_ref[rows, :], c_ref[rows, :]], axis=-1)
        xs.append(x_ref[rows, :] + jnp.dot(mix, wo_ref[...], preferred_element_type=F32))
    for rows, y in zip(subs, _ffn(xs, g2_ref, wg_ref, wu_ref, wd_ref)):
        o_ref[rows, :] = y


def _out_ffn(x, oa, ob, oc, l, p, row0=0, n=None):
    n = x.shape[0] if n is None else n
    tm = TOKEN_TILE
    assert row0 % tm == 0 and n % tm == 0
    row = lambda i: (i + row0 // tm, 0)
    lsel3 = lambda i: (l, 0, 0)
    return pl.pallas_call(
        _out_ffn_kernel,
        grid=(n // tm,),
        in_specs=[
            pl.BlockSpec((tm, D_MODEL), row),
            pl.BlockSpec((tm, NA_W), row),
            pl.BlockSpec((tm, SG_W), row),
            pl.BlockSpec((tm, GQ_W), row),
            _const_spec((None, D_MODEL, D_MODEL), lsel3),
            _const_spec((None, 1, D_MODEL), lsel3),
            _const_spec((None, D_MODEL, D_FF), lsel3),
            _const_spec((None, D_MODEL, D_FF), lsel3),
            _const_spec((None, D_FF, D_MODEL), lsel3),
        ],
        out_specs=pl.BlockSpec((tm, D_MODEL), lambda i: (i, 0)),
        out_shape=jax.ShapeDtypeStruct((n, D_MODEL), F32),
        compiler_params=pltpu.CompilerParams(
            dimension_semantics=("parallel",), vmem_limit_bytes=VMEM_LIMIT_BYTES),
        name="out_ffn",
    )(x, oa, ob, oc, p["w_out"], p["ffn2_norm"], p["ffn2_w_gate"], p["ffn2_w_up"], p["ffn2_w_down"])


def _rope_tables(seq):
    t = np.arange(seq)
    row = (t // GRID_W).astype(np.float32)
    col = (t % GRID_W).astype(np.float32)
    n_freq = HEAD_DIM // 4
    inv = jnp.asarray(ROPE_THETA, F32) ** (-jnp.arange(n_freq, dtype=F32) / n_freq)
    ang = jnp.concatenate([jnp.asarray(row)[:, None] * inv, jnp.asarray(col)[:, None] * inv], axis=-1)
    cos, sin = jnp.cos(ang), jnp.sin(ang)
    reps = LANES // HEAD_DIM
    cos_t = jnp.tile(jnp.concatenate([cos, cos], axis=-1), (1, reps))
    sin_t = jnp.tile(jnp.concatenate([-sin, sin], axis=-1), (1, reps))
    return cos_t, sin_t


def _prepare(ffn1_norm, ffn1_w_gate, ffn1_w_up, ffn1_w_down, mix_norm, w_in, na_q_norm, na_k_norm,
             na_rpb, sg_v_norm, sg_w, sg_b, gqa_q_norm, gqa_k_norm, w_out, ffn2_norm, ffn2_w_gate,
             ffn2_w_up, ffn2_w_down, seq):
    depth = w_in.shape[0]
    cos_t, sin_t = _rope_tables(seq)
    gm = np.kron(np.eye(2 * LANES // HEAD_DIM), np.ones((HEAD_DIM, HEAD_DIM))) / HEAD_DIM
    sg_pairs = jnp.concatenate([sg_w[:, 0::2], sg_w[:, 1::2]], axis=-1).astype(BF16)
    sg_bias = jnp.repeat(jnp.transpose(sg_b, (0, 2, 1)), HEAD_DIM, axis=-1)
    vec = lambda a, reps=1: jnp.tile(a.reshape(depth, 1, -1), (1, 1, reps))
    return dict(
        ffn1_norm=vec(ffn1_norm), ffn1_w_gate=ffn1_w_gate.astype(BF16), ffn1_w_up=ffn1_w_up.astype(BF16),
        ffn1_w_down=ffn1_w_down.astype(BF16), mix_norm=vec(mix_norm), w_in=w_in.astype(BF16),
        gmat=jnp.asarray(gm, BF16), cos=cos_t, sin=sin_t,
        na_q_gain=vec(na_q_norm, NA_HEADS), na_k_gain=vec(na_k_norm, NA_HEADS),
        sg_v_gain=vec(sg_v_norm),
        gqa_q_gain=vec(gqa_q_norm, LANES // HEAD_DIM), gqa_k_gain=vec(gqa_k_norm, LANES // HEAD_DIM),
        sg_w_pairs=sg_pairs, sg_bias=sg_bias,
        na_bias=jax.vmap(_na_bias_table)(na_rpb),
        w_out=w_out.astype(BF16), ffn2_norm=vec(ffn2_norm), ffn2_w_gate=ffn2_w_gate.astype(BF16),
        ffn2_w_up=ffn2_w_up.astype(BF16), ffn2_w_down=ffn2_w_down.astype(BF16),
    )


def _trunk(x_a, x_b, p, depth):
    seq = x_a.shape[1]
    assert x_b.shape[1] == seq and seq % TOKEN_TILE == 0 and seq % (GRID_W * NA_ROWS_PER_STEP) == 0
    n_a, n_b = x_a.shape[0] * seq, x_b.shape[0] * seq
    batch = x_a.shape[0] + x_b.shape[0]
    x = None
    for l in range(depth):
        if l == 0:
            outs = _ffn_proj(x_a.reshape(n_a, D_MODEL), l, p, seq, x2=x_b.reshape(n_b, D_MODEL))
        else:
            outs = _ffn_proj(x, l, p, seq)
        x, qa, ka, va, ob, qc, kc, vct = outs
        oa = _na(qa, ka, va, p["na_bias"][l], batch, seq)
        oc = _gqa(qc, kc, vct, batch, seq)
        if l + 1 < depth:
            x = _out_ffn(x, oa, ob, oc, l, p)
    y_a = _out_ffn(x, oa, ob, oc, depth - 1, p, row0=0, n=n_a)
    y_b = _out_ffn(x, oa, ob, oc, depth - 1, p, row0=n_a, n=n_b)
    return y_a.reshape(x_a.shape), y_b.reshape(x_b.shape)


def kernel(x_prompt, x_sample, ffn1_norm, ffn1_w_gate, ffn1_w_up, ffn1_w_down, mix_norm, w_in, na_q_norm, na_k_norm, na_rpb, sg_v_norm, sg_w, sg_b, gqa_q_norm, gqa_k_norm, w_out, ffn2_norm, ffn2_w_gate, ffn2_w_up, ffn2_w_down):
    p = _prepare(ffn1_norm, ffn1_w_gate, ffn1_w_up, ffn1_w_down, mix_norm, w_in, na_q_norm, na_k_norm,
                 na_rpb, sg_v_norm, sg_w, sg_b, gqa_q_norm, gqa_k_norm, w_out, ffn2_norm, ffn2_w_gate,
                 ffn2_w_up, ffn2_w_down, x_prompt.shape[1])
    return _trunk(x_prompt, x_sample, p, w_in.shape[0])
```

```python
import functools

import numpy as np
import jax
import jax.numpy as jnp
from jax import lax
from jax.experimental import pallas as pl
from jax.experimental.pallas import tpu as pltpu

F32 = jnp.float32
BF16 = jnp.bfloat16

D_MODEL = 1024
HEAD_DIM = 64
GRID_W = 64
NA_HEADS = 4
WIN_R = 8
WIN_C = 16
SG_GROUPS = 4
SG_CHUNK = 128
GQA_Q_HEADS = 8
GQA_KV_HEADS = 2
GQA_REP = GQA_Q_HEADS // GQA_KV_HEADS
ROPE_THETA = 10000.0
D_FF = 2816
EPS = 1e-6
LOG2E = 1.4426950408889634
NEG_BIG = -1e30

NA_W = NA_HEADS * HEAD_DIM
SG_W = SG_GROUPS * HEAD_DIM
GQ_W = GQA_Q_HEADS * HEAD_DIM
GKV_W = GQA_KV_HEADS * HEAD_DIM
D_IN = 3 * NA_W + 2 * SG_W + GQ_W + 2 * GKV_W
NA_KEYS = WIN_R * GRID_W

LANES = 128
VMEM_LIMIT_BYTES = 56 * 1024 * 1024

TOKEN_TILE = 512
TOKEN_SUB_TILES = 2
NA_ROWS_PER_STEP = 16
NA_ROW_GROUP = 4
GQA_TQ = 512
GQA_TK = 256
GQA_CHUNK = 256
GQA_UNROLL = 16
GQA_SKEW = 4
GQA_MAX_SHIFT_GAP = 64.0
BF16_TILE_ROWS = 16
ONES_ROWS = BF16_TILE_ROWS


def _const_spec(shape, index_map):
    return pl.BlockSpec(shape, index_map, pipeline_mode=pl.Buffered(1))


def _rms(x, g):
    ms = jnp.mean(x * x, axis=-1, keepdims=True)
    return x * lax.rsqrt(ms + EPS) * g


def _gelu(x):
    return 0.5 * x * (1.0 + jnp.tanh(0.7978845608028654 * (x + 0.044715 * (x * x * x))))


def _ffn(xs, g_ref, wg_ref, wu_ref, wd_ref):
    hs = []
    for x in xs:
        xn = _rms(x, g_ref[...]).astype(BF16)
        gate = jnp.dot(xn, wg_ref[...], preferred_element_type=F32)
        up = jnp.dot(xn, wu_ref[...], preferred_element_type=F32)
        hs.append((gate / (1.0 + jnp.exp(-gate)) * up).astype(BF16))
    return [x + 0.5 * jnp.dot(h, wd_ref[...], preferred_element_type=F32) for x, h in zip(xs, hs)]


def _sub_tiles(tm):
    rows = tm // TOKEN_SUB_TILES
    return [slice(s * rows, (s + 1) * rows) for s in range(TOKEN_SUB_TILES)]


def _group_mean_sq(z, gmat):
    sq = z * z
    hi = sq.astype(BF16)
    lo = (sq - hi.astype(F32)).astype(BF16)
    return (jnp.dot(hi, gmat, preferred_element_type=F32)
            + jnp.dot(lo, gmat, preferred_element_type=F32))


def _head_norm(z, gain, gmat):
    return z * lax.rsqrt(_group_mean_sq(z, gmat) + EPS) * gain


def _rope(x, cos, sin, first_half):
    rot = jnp.where(first_half, pltpu.roll(x, LANES - HEAD_DIM // 2, 1), pltpu.roll(x, HEAD_DIM // 2, 1))
    return x * cos + rot * sin


def _project_rows(z, rows, gmat_ref, cos_ref, sin_ref, naq_ref, nak_ref, sgv_ref, gq_ref, gk_ref, sgw_ref, sgb_ref,
                  qa_ref, ka_ref, va_ref, ob_ref, qct_ref, kc_ref, vct_ref):
    r = z.shape[0]
    gmat = gmat_ref[...]
    gmat_half = gmat_ref[0:LANES, 0:LANES]

    c0 = 0
    qa = _head_norm(z[:, c0:c0 + NA_W], naq_ref[...], gmat)
    qa_ref[rows, :] = (qa * (HEAD_DIM ** -0.5 * LOG2E)).astype(BF16)
    c0 += NA_W
    ka_ref[rows, :] = _head_norm(z[:, c0:c0 + NA_W], nak_ref[...], gmat).astype(BF16)
    c0 += NA_W
    va_ref[rows, :] = z[:, c0:c0 + NA_W].astype(BF16)
    c0 += NA_W
    u = _gelu(z[:, c0:c0 + SG_W])
    c0 += SG_W
    vn = _head_norm(_gelu(z[:, c0:c0 + SG_W]), sgv_ref[...], gmat)
    c0 += SG_W

    lane = lax.broadcasted_iota(jnp.int32, (r, LANES), 1)
    first_half = (lane % HEAD_DIM) < (HEAD_DIM // 2)
    cos = cos_ref[rows, :]
    sin = sin_ref[rows, :]
    gq = jnp.concatenate([gq_ref[...], gq_ref[...]], axis=-1)
    for s2 in range(GQ_W // (2 * LANES)):
        qn2 = _head_norm(z[:, c0 + s2 * 2 * LANES:c0 + (s2 + 1) * 2 * LANES], gq, gmat)
        for s in range(2):
            qr = _rope(qn2[:, s * LANES:(s + 1) * LANES], cos, sin, first_half) * (HEAD_DIM ** -0.5 * LOG2E)
            qct_ref[(2 * s2 + s) * LANES:(2 * s2 + s + 1) * LANES, rows] = qr.T.astype(BF16)
    c0 += GQ_W
    kn = _head_norm(z[:, c0:c0 + GKV_W], gk_ref[...], gmat_half)
    kc_ref[rows, :] = _rope(kn, cos, sin, first_half).astype(BF16)
    c0 += GKV_W
    vct_ref[:, rows] = z[:, c0:c0 + GKV_W].T.astype(BF16)

    lane_c = lax.broadcasted_iota(jnp.int32, (SG_CHUNK, LANES), 1)
    lo = lane_c < HEAD_DIM
    for ch in range(r // SG_CHUNK):
        cr = slice(ch * SG_CHUNK, (ch + 1) * SG_CHUNK)
        mixed = []
        for pair in range(SG_GROUPS // 2):
            v128 = vn[cr, pair * LANES:(pair + 1) * LANES]
            rhs = jnp.concatenate([jnp.where(lo, v128, 0.0), jnp.where(lo, 0.0, v128)], axis=0)
            mixed.append(jnp.dot(sgw_ref[pair], rhs.astype(BF16), preferred_element_type=F32))
        mixed = jnp.concatenate(mixed, axis=-1) + sgb_ref[...]
        ob_ref[rows.start + ch * SG_CHUNK:rows.start + (ch + 1) * SG_CHUNK, :] = (u[cr, :] * mixed).astype(BF16)


def _ffn_proj_kernel(*refs, first_tiles):
    if first_tiles is None:
        x_ref, refs = refs[0], refs[1:]
        load = lambda rows: x_ref[rows, :]
    else:
        x_ref, x2_ref, refs = refs[0], refs[1], refs[2:]
        from_first = pl.program_id(0) < first_tiles
        load = lambda rows: jnp.where(from_first, x_ref[rows, :], x2_ref[rows, :])
    g1_ref, wg_ref, wu_ref, wd_ref, gm_ref, win_ref = refs[:6]
    tables, xo_ref, outs = refs[6:16], refs[16], refs[17:]
    subs = _sub_tiles(xo_ref.shape[0])
    xs = _ffn([load(rows) for rows in subs], g1_ref, wg_ref, wu_ref, wd_ref)
    zs = []
    for rows, x in zip(subs, xs):
        xo_ref[rows, :] = x
        h = _rms(x, gm_ref[...]).astype(BF16)
        zs.append(jnp.dot(h, win_ref[...], preferred_element_type=F32))
    for rows, z in zip(subs, zs):
        _project_rows(z, rows, *tables, *outs)


def _ffn_proj(x, l, p, seq, x2=None):
    tm = TOKEN_TILE
    n = x.shape[0] + (0 if x2 is None else x2.shape[0])
    nt = seq // tm
    row = lambda i: (i, 0)
    const2 = lambda i: (0, 0)
    lsel3 = lambda i: (l, 0, 0)
    if x2 is None:
        first_tiles = None
        x_args = (x,)
        x_specs = [pl.BlockSpec((tm, D_MODEL), row)]
    else:
        first_tiles = x.shape[0] // tm
        x_args = (x, x2)
        x_specs = [pl.BlockSpec((tm, D_MODEL), lambda i: (jnp.minimum(i, first_tiles - 1), 0)),
                   pl.BlockSpec((tm, D_MODEL), lambda i: (jnp.maximum(i - first_tiles, 0), 0))]
    in_specs = x_specs + [
        _const_spec((None, 1, D_MODEL), lsel3),
        _const_spec((None, D_MODEL, D_FF), lsel3),
        _const_spec((None, D_MODEL, D_FF), lsel3),
        _const_spec((None, D_FF, D_MODEL), lsel3),
        _const_spec((None, 1, D_MODEL), lsel3),
        _const_spec((None, D_MODEL, D_IN), lsel3),
        _const_spec((2 * LANES, 2 * LANES), const2),
        pl.BlockSpec((tm, LANES), lambda i: (i % nt, 0)),
        pl.BlockSpec((tm, LANES), lambda i: (i % nt, 0)),
        _const_spec((None, 1, NA_W), lsel3),
        _const_spec((None, 1, NA_W), lsel3),
        _const_spec((None, 1, SG_W), lsel3),
        _const_spec((None, 1, LANES), lsel3),
        _const_spec((None, 1, LANES), lsel3),
        _const_spec((None, SG_GROUPS // 2, SG_CHUNK, 2 * SG_CHUNK), lambda i: (l, 0, 0, 0)),
        _const_spec((None, SG_CHUNK, SG_W), lsel3),
    ]
    out_shape = [
        jax.ShapeDtypeStruct((n, D_MODEL), F32),
        jax.ShapeDtypeStruct((n, NA_W), BF16),
        jax.ShapeDtypeStruct((n, NA_W), BF16),
        jax.ShapeDtypeStruct((n, NA_W), BF16),
        jax.ShapeDtypeStruct((n, SG_W), BF16),
        jax.ShapeDtypeStruct((GQ_W, n), BF16),
        jax.ShapeDtypeStruct((n, GKV_W), BF16),
        jax.ShapeDtypeStruct((GKV_W, n), BF16),
    ]
    out_specs = [
        pl.BlockSpec((tm, D_MODEL), row),
        pl.BlockSpec((tm, NA_W), row),
        pl.BlockSpec((tm, NA_W), row),
        pl.BlockSpec((tm, NA_W), row),
        pl.BlockSpec((tm, SG_W), row),
        pl.BlockSpec((GQ_W, tm), lambda i: (0, i)),
        pl.BlockSpec((tm, GKV_W), row),
        pl.BlockSpec((GKV_W, tm), lambda i: (0, i)),
    ]
    return pl.pallas_call(
        functools.partial(_ffn_proj_kernel, first_tiles=first_tiles),
        grid=(n // tm,),
        in_specs=in_specs,
        out_specs=out_specs,
        out_shape=out_shape,
        compiler_params=pltpu.CompilerParams(
            dimension_semantics=("parallel",), vmem_limit_bytes=VMEM_LIMIT_BYTES),
        name="ffn_proj",
    )(*x_args, p["ffn1_norm"], p["ffn1_w_gate"], p["ffn1_w_up"], p["ffn1_w_down"], p["mix_norm"], p["w_in"],
      p["gmat"], p["cos"], p["sin"], p["na_q_gain"], p["na_k_gain"], p["sg_v_gain"], p["gqa_q_gain"],
      p["gqa_k_gain"], p["sg_w_pairs"], p["sg_bias"])


def _na_kernel(q_ref, k_ref, v_ref, bias_ref, o_ref, *, rows_per_step, grid_rows):
    i = pl.program_id(1)
    lane_head = lax.broadcasted_iota(jnp.int32, (GRID_W, NA_W), 1) // HEAD_DIM
    head_mask = [lane_head == h for h in range(NA_HEADS)]

    for group0 in range(0, rows_per_step, NA_ROW_GROUP):
        rows = []
        for rr in range(group0, group0 + NA_ROW_GROUP):
            r = i * rows_per_step + rr
            rs = jnp.clip(r - WIN_R // 2, 0, grid_rows - WIN_R)
            rows.append((rr * GRID_W, pl.multiple_of(rs * GRID_W, GRID_W), r - rs))

        scores = []
        for q0, k0, t in rows:
            q_row = q_ref[q0:q0 + GRID_W, :]
            zero = jnp.zeros_like(q_row)
            q_blk = jnp.concatenate([jnp.where(head_mask[h], q_row, zero) for h in range(NA_HEADS)], axis=0)
            kw = k_ref[pl.ds(k0, NA_KEYS), :]
            scores.append(lax.dot_general(q_blk, kw, (((1,), (1,)), ((), ())), preferred_element_type=F32))

        probs = []
        for (q0, k0, t), s in zip(rows, scores):
            s = s + bias_ref[t]
            m = jnp.max(s, axis=-1, keepdims=True)
            e = jnp.exp2(s - m)
            probs.append((e.astype(BF16), jnp.sum(e, axis=-1, keepdims=True)))

        for (q0, k0, t), (e, denom) in zip(rows, probs):
            vw = v_ref[pl.ds(k0, NA_KEYS), :]
            pv = jnp.dot(e, vw, preferred_element_type=F32) / denom
            out = jnp.where(head_mask[0], pv[0:GRID_W], 0.0)
            for h in range(1, NA_HEADS):
                out = jnp.where(head_mask[h], pv[h * GRID_W:(h + 1) * GRID_W], out)
            o_ref[q0:q0 + GRID_W, :] = out.astype(BF16)


def _na(qa, ka, va, bias_l, batch, seq):
    n = qa.shape[0]
    grid_rows = seq // GRID_W
    rps = NA_ROWS_PER_STEP
    steps = grid_rows // rps
    blk = rps * GRID_W
    return pl.pallas_call(
        functools.partial(_na_kernel, rows_per_step=rps, grid_rows=grid_rows),
        grid=(batch, steps),
        in_specs=[
            pl.BlockSpec((blk, NA_W), lambda b, i: (b * steps + i, 0)),
            pl.BlockSpec((seq, NA_W), lambda b, i: (b, 0)),
            pl.BlockSpec((seq, NA_W), lambda b, i: (b, 0)),
            _const_spec((WIN_R, NA_HEADS * GRID_W, NA_KEYS), lambda b, i: (0, 0, 0)),
        ],
        out_specs=pl.BlockSpec((blk, NA_W), lambda b, i: (b * steps + i, 0)),
        out_shape=jax.ShapeDtypeStruct((n, NA_W), BF16),
        compiler_params=pltpu.CompilerParams(
            dimension_semantics=("parallel", "parallel"), vmem_limit_bytes=VMEM_LIMIT_BYTES),
        name="na",
    )(qa, ka, va, bias_l)


def _na_bias_table(rpb):
    c = np.arange(GRID_W)
    kc = np.arange(GRID_W)
    cs = np.clip(c - WIN_C // 2, 0, GRID_W - WIN_C)
    valid = (kc[None, :] >= cs[:, None]) & (kc[None, :] < cs[:, None] + WIN_C)
    dc = kc[None, :] - c[:, None] + (WIN_C - 1)
    t = np.arange(WIN_R)
    i = np.arange(WIN_R)
    dr = i[None, :] - t[:, None] + (WIN_R - 1)
    pick_col = ((dc[None] == np.arange(2 * WIN_C - 1)[:, None, None]) & valid[None]).astype(np.float32)
    pick_row = (dr[None] == np.arange(2 * WIN_R - 1)[:, None, None]).astype(np.float32)
    tbl = jnp.einsum("hrd,rti,dck->thcik", rpb.astype(F32), pick_row, pick_col,
                     precision=lax.Precision.HIGHEST) * LOG2E
    tbl = jnp.where(valid[None, None, :, None, :], tbl, NEG_BIG)
    return tbl.reshape(WIN_R, NA_HEADS * GRID_W, NA_KEYS)


def _gqa_kernel(q_ref, k_ref, vt_ref, o_ref, qt_scr, s_scr, mc_scr, m_scr, acc_scr, gap_scr, *, tq, tk, seq):
    g = pl.program_id(1)
    m_lanes = GQA_REP * tq
    n_kv = seq // tk

    q = q_ref[...]
    qt = jnp.concatenate([q[h * HEAD_DIM:(h + 1) * HEAD_DIM, :] for h in range(GQA_REP)], axis=1)
    zero = jnp.zeros_like(qt)
    qt_scr[...] = jnp.concatenate([jnp.where(g == 0, qt, zero), jnp.where(g == 1, qt, zero)], axis=0)
    ones = jnp.ones((ONES_ROWS, tk), BF16)
    v_row0 = pl.multiple_of(g * HEAD_DIM, HEAD_DIM)
    chunks = [slice(c * GQA_CHUNK, (c + 1) * GQA_CHUNK) for c in range(m_lanes // GQA_CHUNK)]

    def load_k(j):
        return k_ref[pl.ds(pl.multiple_of(j * tk, tk), tk), :]

    def load_v_ones(j):
        vt = vt_ref[pl.ds(v_row0, HEAD_DIM), pl.ds(pl.multiple_of(j * tk, tk), tk)]
        return jnp.concatenate([vt, ones], axis=0)

    acc_scr[...] = jnp.zeros(acc_scr.shape, F32)
    gap_scr[...] = jnp.zeros(gap_scr.shape, F32)
    for cols in chunks:
        s_first = jnp.dot(k_ref[0:BF16_TILE_ROWS, :], qt_scr[:, cols], preferred_element_type=F32)
        m_scr[:, cols] = jnp.max(s_first, axis=0, keepdims=True)

    def single_pass_tiles(jj, carry):
        pending = []
        for u in range(GQA_UNROLL):
            kj, vext = load_k(jj * GQA_UNROLL + u), load_v_ones(jj * GQA_UNROLL + u)
            for cols in chunks:
                st = jnp.dot(kj, qt_scr[:, cols], preferred_element_type=F32)

                def consume(st=st, vext=vext, cols=cols):
                    shift = m_scr[:, cols]
                    tile_max = jnp.max(st, axis=0, keepdims=True)
                    pt = jnp.exp2(st - shift).astype(BF16)
                    m_new = jnp.maximum(shift, tile_max)
                    acc_scr[:, cols] = ((acc_scr[:, cols] + jnp.dot(vext, pt, preferred_element_type=F32))
                                        * jnp.exp2(shift - m_new))
                    m_scr[:, cols] = m_new
                    gap_scr[:, cols] = jnp.maximum(gap_scr[:, cols], tile_max - shift)

                pending.append(consume)
                if len(pending) > GQA_SKEW:
                    pending.pop(0)()
        for consume in pending:
            consume()
        return carry

    lax.fori_loop(0, n_kv // GQA_UNROLL, single_pass_tiles, 0)

    def scores(j, slot):
        kj = load_k(j)
        for c, cols in enumerate(chunks):
            st = jnp.dot(kj, qt_scr[:, cols], preferred_element_type=F32)
            s_scr[slot, c] = st
            mc_scr[slot, :, cols] = jnp.max(st, axis=0, keepdims=True)

    def softmax_pv(j, slot):
        vext = load_v_ones(j)
        for c, cols in enumerate(chunks):
            m_old = m_scr[:, cols]
            m_new = jnp.maximum(m_old, mc_scr[slot, :, cols])
            alpha = jnp.exp2(m_old - m_new)
            pt = jnp.exp2(s_scr[slot, c] - m_new).astype(BF16)
            acc_scr[:, cols] = acc_scr[:, cols] * alpha + jnp.dot(vext, pt, preferred_element_type=F32)
            m_scr[:, cols] = m_new

    @pl.when(jnp.max(gap_scr[...]) > GQA_MAX_SHIFT_GAP)
    def _recompute_two_pass():
        m_scr[...] = jnp.full(m_scr.shape, NEG_BIG, F32)
        acc_scr[...] = jnp.zeros(acc_scr.shape, F32)
        scores(0, 0)

        def two_pass_tiles(jj, carry):
            for u in range(GQA_UNROLL // 2):
                j0 = GQA_UNROLL * jj + 2 * u
                scores(j0 + 1, 1)
                softmax_pv(j0, 0)
                scores(jnp.minimum(j0 + 2, n_kv - 1), 0)
                softmax_pv(j0 + 1, 1)
            return carry

        lax.fori_loop(0, n_kv // GQA_UNROLL, two_pass_tiles, 0)

    inv_denom = 1.0 / acc_scr[HEAD_DIM:HEAD_DIM + 1, :]
    for h in range(GQA_REP):
        lanes = slice(h * tq, (h + 1) * tq)
        o_ref[h * HEAD_DIM:(h + 1) * HEAD_DIM, :] = (acc_scr[0:HEAD_DIM, lanes] * inv_denom[:, lanes]).astype(BF16)


def _gqa(qct, kc, vct, batch, seq):
    n = qct.shape[1]
    tq, tk = GQA_TQ, GQA_TK
    assert seq % tq == 0 and seq % (tk * GQA_UNROLL) == 0
    nq = seq // tq
    qw = GQA_REP * HEAD_DIM
    return pl.pallas_call(
        functools.partial(_gqa_kernel, tq=tq, tk=tk, seq=seq),
        grid=(batch, GQA_KV_HEADS, nq),
        in_specs=[
            pl.BlockSpec((qw, tq), lambda b, g, i: (g, b * nq + i)),
            pl.BlockSpec((seq, GKV_W), lambda b, g, i: (b, 0)),
            pl.BlockSpec((GKV_W, seq), lambda b, g, i: (0, b)),
        ],
        out_specs=pl.BlockSpec((qw, tq), lambda b, g, i: (g, b * nq + i)),
        out_shape=jax.ShapeDtypeStruct((GQ_W, n), BF16),
        scratch_shapes=[
            pltpu.VMEM((GKV_W, GQA_REP * tq), BF16),
            pltpu.VMEM((2, GQA_REP * tq // GQA_CHUNK, tk, GQA_CHUNK), F32),
            pltpu.VMEM((2, 1, GQA_REP * tq), F32),
            pltpu.VMEM((1, GQA_REP * tq), F32),
            pltpu.VMEM((HEAD_DIM + ONES_ROWS, GQA_REP * tq), F32),
            pltpu.VMEM((1, GQA_REP * tq), F32),
        ],
        compiler_params=pltpu.CompilerParams(
            dimension_semantics=("parallel", "parallel", "parallel"), vmem_limit_bytes=VMEM_LIMIT_BYTES),
        name="gqa",
    )(qct, kc, vct)


def _out_ffn_kernel(x_ref, a_ref, b_ref, ct_ref, wo_ref, g2_ref, wg_ref, wu_ref, wd_ref, o_ref):
    subs = _sub_tiles(o_ref.shape[0])
    xs = []
    for rows in subs:
        c = ct_ref[:, rows].astype(F32).T.astype(BF16)
        mix = jnp.concatenate([a_ref[rows, :], b_ref[rows, :], c], axis=-1)
        xs.append(x_ref[rows, :] + jnp.dot(mix, wo_ref[...], preferred_element_type=F32))
    for rows, y in zip(subs, _ffn(xs, g2_ref, wg_ref, wu_ref, wd_ref)):
        o_ref[rows, :] = y


def _out_ffn(x, oa, ob, oct, l, p, row0=0, n=None):
    n = x.shape[0] if n is None else n
    tm = TOKEN_TILE
    assert row0 % tm == 0 and n % tm == 0
    row = lambda i: (i + row0 // tm, 0)
    lsel3 = lambda i: (l, 0, 0)
    return pl.pallas_call(
        _out_ffn_kernel,
        grid=(n // tm,),
        in_specs=[
            pl.BlockSpec((tm, D_MODEL), row),
            pl.BlockSpec((tm, NA_W), row),
            pl.BlockSpec((tm, SG_W), row),
            pl.BlockSpec((GQ_W, tm), lambda i: (0, i + row0 // tm)),
            _const_spec((None, D_MODEL, D_MODEL), lsel3),
            _const_spec((None, 1, D_MODEL), lsel3),
            _const_spec((None, D_MODEL, D_FF), lsel3),
            _const_spec((None, D_MODEL, D_FF), lsel3),
            _const_spec((None, D_FF, D_MODEL), lsel3),
        ],
        out_specs=pl.BlockSpec((tm, D_MODEL), lambda i: (i, 0)),
        out_shape=jax.ShapeDtypeStruct((n, D_MODEL), F32),
        compiler_params=pltpu.CompilerParams(
            dimension_semantics=("parallel",), vmem_limit_bytes=VMEM_LIMIT_BYTES),
        name="out_ffn",
    )(x, oa, ob, oct, p["w_out"], p["ffn2_norm"], p["ffn2_w_gate"], p["ffn2_w_up"], p["ffn2_w_down"])


def _rope_tables(seq):
    t = np.arange(seq)
    row = (t // GRID_W).astype(np.float32)
    col = (t % GRID_W).astype(np.float32)
    n_freq = HEAD_DIM // 4
    inv = jnp.asarray(ROPE_THETA, F32) ** (-jnp.arange(n_freq, dtype=F32) / n_freq)
    ang = jnp.concatenate([jnp.asarray(row)[:, None] * inv, jnp.asarray(col)[:, None] * inv], axis=-1)
    cos, sin = jnp.cos(ang), jnp.sin(ang)
    reps = LANES // HEAD_DIM
    cos_t = jnp.tile(jnp.concatenate([cos, cos], axis=-1), (1, reps))
    sin_t = jnp.tile(jnp.concatenate([-sin, sin], axis=-1), (1, reps))
    return cos_t, sin_t


def _prepare(ffn1_norm, ffn1_w_gate, ffn1_w_up, ffn1_w_down, mix_norm, w_in, na_q_norm, na_k_norm,
             na_rpb, sg_v_norm, sg_w, sg_b, gqa_q_norm, gqa_k_norm, w_out, ffn2_norm, ffn2_w_gate,
             ffn2_w_up, ffn2_w_down, seq):
    depth = w_in.shape[0]
    cos_t, sin_t = _rope_tables(seq)
    gm = np.kron(np.eye(2 * LANES // HEAD_DIM), np.ones((HEAD_DIM, HEAD_DIM))) / HEAD_DIM
    sg_pairs = jnp.concatenate([sg_w[:, 0::2], sg_w[:, 1::2]], axis=-1).astype(BF16)
    sg_bias = jnp.repeat(jnp.transpose(sg_b, (0, 2, 1)), HEAD_DIM, axis=-1)
    vec = lambda a, reps=1: jnp.tile(a.reshape(depth, 1, -1), (1, 1, reps))
    return dict(
        ffn1_norm=vec(ffn1_norm), ffn1_w_gate=ffn1_w_gate.astype(BF16), ffn1_w_up=ffn1_w_up.astype(BF16),
        ffn1_w_down=ffn1_w_down.astype(BF16), mix_norm=vec(mix_norm), w_in=w_in.astype(BF16),
        gmat=jnp.asarray(gm, BF16), cos=cos_t, sin=sin_t,
        na_q_gain=vec(na_q_norm, NA_HEADS), na_k_gain=vec(na_k_norm, NA_HEADS),
        sg_v_gain=vec(sg_v_norm),
        gqa_q_gain=vec(gqa_q_norm, LANES // HEAD_DIM), gqa_k_gain=vec(gqa_k_norm, LANES // HEAD_DIM),
        sg_w_pairs=sg_pairs, sg_bias=sg_bias,
        na_bias=jax.vmap(_na_bias_table)(na_rpb),
        w_out=w_out.astype(BF16), ffn2_norm=vec(ffn2_norm), ffn2_w_gate=ffn2_w_gate.astype(BF16),
        ffn2_w_up=ffn2_w_up.astype(BF16), ffn2_w_down=ffn2_w_down.astype(BF16),
    )


def _trunk(x_a, x_b, p, depth):
    seq = x_a.shape[1]
    assert x_b.shape[1] == seq and seq % TOKEN_TILE == 0 and seq % (GRID_W * NA_ROWS_PER_STEP) == 0
    n_a, n_b = x_a.shape[0] * seq, x_b.shape[0] * seq
    batch = x_a.shape[0] + x_b.shape[0]
    x = None
    for l in range(depth):
        if l == 0:
            outs = _ffn_proj(x_a.reshape(n_a, D_MODEL), l, p, seq, x2=x_b.reshape(n_b, D_MODEL))
        else:
            outs = _ffn_proj(x, l, p, seq)
        x, qa, ka, va, ob, qct, kc, vct = outs
        oa = _na(qa, ka, va, p["na_bias"][l], batch, seq)
        oct = _gqa(qct, kc, vct, batch, seq)
        if l + 1 < depth:
            x = _out_ffn(x, oa, ob, oct, l, p)
    y_a = _out_ffn(x, oa, ob, oct, depth - 1, p, row0=0, n=n_a)
    y_b = _out_ffn(x, oa, ob, oct, depth - 1, p, row0=n_a, n=n_b)
    return y_a.reshape(x_a.shape), y_b.reshape(x_b.shape)


def kernel(x_prompt, x_sample, ffn1_norm, ffn1_w_gate, ffn1_w_up, ffn1_w_down, mix_norm, w_in, na_q_norm, na_k_norm, na_rpb, sg_v_norm, sg_w, sg_b, gqa_q_norm, gqa_k_norm, w_out, ffn2_norm, ffn2_w_gate, ffn2_w_up, ffn2_w_down):
    p = _prepare(ffn1_norm, ffn1_w_gate, ffn1_w_up, ffn1_w_down, mix_norm, w_in, na_q_norm, na_k_norm,
                 na_rpb, sg_v_norm, sg_w, sg_b, gqa_q_norm, gqa_k_norm, w_out, ffn2_norm, ffn2_w_gate,
                 ffn2_w_up, ffn2_w_down, x_prompt.shape[1])
    return _trunk(x_prompt, x_sample, p, w_in.shape[0])
```
